```python
import jax, jax.numpy as jnp
from jax import lax
import numpy as np

D_MODEL = 1024
BATCH = 8
SEQ = 4096
DEPTH = 1

CHUNK = 128
SGU_GROUPS = 8
SGU_GROUP_DIM = D_MODEL // SGU_GROUPS
SGU_WIDTH = SGU_GROUPS * SGU_GROUP_DIM
ATTN_HEADS = 8
ATTN_HEAD_DIM = 128
ATTN_WIDTH = ATTN_HEADS * ATTN_HEAD_DIM
IDX_HEADS = 8
IDX_DIM = 64
IDX_TOPK_MAX = 256
QUERY_BLOCK = 64
FFN_DIM = 4 * D_MODEL
ALPHA = (2.0 * DEPTH) ** 0.25
BETA = (8.0 * DEPTH) ** -0.25
LN_EPS = 1e-5
IN_SPLITS = (SGU_WIDTH, SGU_WIDTH, ATTN_WIDTH, ATTN_WIDTH, ATTN_WIDTH,
             IDX_HEADS * IDX_DIM, IDX_DIM, IDX_HEADS, D_MODEL, D_MODEL)
IN_WIDTH = sum(IN_SPLITS)

kernel_name = "hybrid_gmlp_dsa_gated_deepnorm"


def layer_norm(x, g, b):
    xf = x.astype(jnp.float32)
    mu = jnp.mean(xf, axis=-1, keepdims=True)
    xc = xf - mu
    var = jnp.mean(xc * xc, axis=-1, keepdims=True)
    return (xc * lax.rsqrt(var + LN_EPS) * g.astype(jnp.float32) + b.astype(jnp.float32)).astype(x.dtype)


def sgu_mixer(u, v, ln_g, ln_b, w_s, b_s):
    bsz, seq, _ = v.shape
    u = jax.nn.gelu(u)
    v = layer_norm(jax.nn.gelu(v), ln_g, ln_b)
    v = v.reshape(bsz, seq // CHUNK, CHUNK, SGU_GROUPS, SGU_GROUP_DIM)
    causal = jnp.tril(jnp.ones((CHUNK, CHUNK), dtype=bool))
    w = jnp.where(causal[None], w_s, jnp.zeros_like(w_s))
    s = jnp.einsum('gts,bnsgc->bntgc', w, v) + b_s.T[None, None, :, :, None]
    return u * s.reshape(bsz, seq, SGU_WIDTH)


def dsa_mixer(q, k, v, q_idx, k_idx, w_idx):
    bsz, seq, _ = q.shape
    top_k = min(IDX_TOPK_MAX, seq // 4)
    q = q.reshape(bsz, seq, ATTN_HEADS, ATTN_HEAD_DIM)
    k = k.reshape(bsz, seq, ATTN_HEADS, ATTN_HEAD_DIM)
    v = v.reshape(bsz, seq, ATTN_HEADS, ATTN_HEAD_DIM)
    q_idx = q_idx.reshape(bsz, seq, IDX_HEADS, IDX_DIM)
    w_idx = w_idx * (IDX_HEADS ** -0.5 * IDX_DIM ** -0.5)
    key_pos = jnp.arange(seq)
    gather = jax.vmap(lambda arr, ids: arr[ids])

    def block(start):
        qb = lax.dynamic_slice_in_dim(q, start, QUERY_BLOCK, axis=1)
        qib = lax.dynamic_slice_in_dim(q_idx, start, QUERY_BLOCK, axis=1)
        wb = lax.dynamic_slice_in_dim(w_idx, start, QUERY_BLOCK, axis=1)
        q_pos = start + jnp.arange(QUERY_BLOCK)
        causal = key_pos[None, :] <= q_pos[:, None]
        logits = jnp.einsum('bthd,bsd->bths', qib, k_idx)
        score = jnp.einsum('bth,bths->bts', wb, jax.nn.relu(logits)).astype(jnp.float32)
        score = jnp.where(causal[None], score, -jnp.inf)
        _, idx = lax.top_k(score, top_k)
        valid = idx <= q_pos[None, :, None]
        kg = gather(k, idx)
        vg = gather(v, idx)
        att = jnp.einsum('bthd,btkhd->bthk', qb, kg).astype(jnp.float32) * (ATTN_HEAD_DIM ** -0.5)
        att = jnp.where(valid[:, :, None, :], att, -jnp.inf)
        p = jax.nn.softmax(att, axis=-1).astype(v.dtype)
        return jnp.einsum('bthk,btkhd->bthd', p, vg)

    starts = jnp.arange(0, seq, QUERY_BLOCK)
    out = lax.map(block, starts)
    return out.transpose(1, 0, 2, 3, 4).reshape(bsz, seq, ATTN_WIDTH)


def hybrid_layer(x, w_in, sgu_ln_g, sgu_ln_b, sgu_w, sgu_b, w_branch_a, w_branch_b, w_out,
                 ln1_g, ln1_b, w_ffn_up, w_ffn_down, ln2_g, ln2_b):
    proj = x @ w_in
    offsets = [int(o) for o in np.cumsum(IN_SPLITS)[:-1]]
    u_a, v_a, q, k, v, q_idx, k_idx, w_idx, g_a, g_b = jnp.split(proj, offsets, axis=-1)
    a = sgu_mixer(u_a, v_a, sgu_ln_g, sgu_ln_b, sgu_w, sgu_b)
    b = dsa_mixer(q, k, v, q_idx, k_idx, w_idx)
    merged = jax.nn.sigmoid(g_a) * (a @ w_branch_a) + jax.nn.sigmoid(g_b) * (b @ w_branch_b)
    x = layer_norm(ALPHA * x + merged @ w_out, ln1_g, ln1_b)
    h = jnp.square(jax.nn.relu(x @ w_ffn_up))
    return layer_norm(ALPHA * x + h @ w_ffn_down, ln2_g, ln2_b)


def setup_inputs(seed: int = 0) -> dict:
    key = jax.random.key(seed)
    ks = jax.random.split(key, 16)
    n = lambda k, shape: jax.random.normal(k, shape, dtype=jnp.float32)
    L = DEPTH
    return {
        "x": n(ks[0], (BATCH, SEQ, D_MODEL)),
        "w_in": n(ks[1], (L, D_MODEL, IN_WIDTH)) * D_MODEL ** -0.5,
        "sgu_ln_g": 1.0 + 0.02 * n(ks[2], (L, SGU_WIDTH)),
        "sgu_ln_b": 0.02 * n(ks[3], (L, SGU_WIDTH)),
        "sgu_w": n(ks[4], (L, SGU_GROUPS, CHUNK, CHUNK)) * CHUNK ** -0.5,
        "sgu_b": 1.0 + 0.02 * n(ks[5], (L, SGU_GROUPS, CHUNK)),
        "w_branch_a": n(ks[6], (L, D_MODEL, D_MODEL)) * (D_MODEL ** -0.5 * BETA),
        "w_branch_b": n(ks[7], (L, ATTN_WIDTH, D_MODEL)) * (ATTN_WIDTH ** -0.5 * BETA),
        "w_out": n(ks[8], (L, D_MODEL, D_MODEL)) * (D_MODEL ** -0.5 * BETA),
        "ln1_g": 1.0 + 0.02 * n(ks[9], (L, D_MODEL)),
        "ln1_b": 0.02 * n(ks[10], (L, D_MODEL)),
        "w_ffn_up": n(ks[11], (L, D_MODEL, FFN_DIM)) * D_MODEL ** -0.5,
        "w_ffn_down": n(ks[12], (L, FFN_DIM, D_MODEL)) * (FFN_DIM ** -0.5 * BETA),
        "ln2_g": 1.0 + 0.02 * n(ks[13], (L, D_MODEL)),
        "ln2_b": 0.02 * n(ks[14], (L, D_MODEL)),
    }


def reference(x, w_in, sgu_ln_g, sgu_ln_b, sgu_w, sgu_b, w_branch_a, w_branch_b, w_out,
              ln1_g, ln1_b, w_ffn_up, w_ffn_down, ln2_g, ln2_b):
    for i in range(DEPTH):
        x = hybrid_layer(x, w_in[i], sgu_ln_g[i], sgu_ln_b[i], sgu_w[i], sgu_b[i],
                         w_branch_a[i], w_branch_b[i], w_out[i], ln1_g[i], ln1_b[i],
                         w_ffn_up[i], w_ffn_down[i], ln2_g[i], ln2_b[i])
    return x
```

```python
import functools

import jax
import jax.numpy as jnp
from jax import lax
from jax.experimental import pallas as pl
from jax.experimental.pallas import tpu as pltpu

D_MODEL = 1024
CHUNK = 128
SGU_GROUPS = 8
SGU_GROUP_DIM = D_MODEL // SGU_GROUPS
ATTN_HEADS = 8
ATTN_HEAD_DIM = 128
ATTN_WIDTH = ATTN_HEADS * ATTN_HEAD_DIM
IDX_HEADS = 8
IDX_DIM = 64
IDX_TOPK_MAX = 256
FFN_DIM = 4 * D_MODEL
LN_EPS = 1e-5
IN_SPLITS = (D_MODEL, D_MODEL, ATTN_WIDTH, ATTN_WIDTH, ATTN_WIDTH,
             IDX_HEADS * IDX_DIM, IDX_DIM, IDX_HEADS, D_MODEL, D_MODEL)

LANES = 128
VMEM_LIMIT_BYTES = 56 * 2**20

_MXU_DTYPE = jnp.bfloat16
_NEG = -1e30

_C_U, _C_V, _C_Q, _C_K, _C_VA, _C_GA, _C_GB = (i * D_MODEL for i in range(7))
_C_QI = 7 * D_MODEL
_C_KW = _C_QI + IDX_HEADS * IDX_DIM
_W_PACKED = _C_KW + LANES

_TM_PROJ = 256
_TQ_IDX = 128
_SK_IDX = 512
_TQ_ATT = 256
_SK_ATT = 512
_TM_OUT = 256
_FFN_CHUNK = 1024


def _layer_norm(x, g, b):
    mu = jnp.mean(x, axis=-1, keepdims=True)
    xc = x - mu
    var = jnp.mean(xc * xc, axis=-1, keepdims=True)
    return xc * lax.rsqrt(var + LN_EPS) * g + b


def _resident(shape):
    nd = len(shape)
    return pl.BlockSpec(shape, lambda *_: (0,) * nd, pipeline_mode=pl.Buffered(1))


def _proj_kernel(x_ref, w_ref, lng_ref, lnb_ref, sw_ref, sbt_ref, wa_ref,
                 q_ref, k_ref, v_ref, qi_ref, kw_ref, ma_ref, gb_ref,
                 u_sc, vn_sc, a_sc):
    tm = x_ref.shape[0]
    xb = x_ref[...].astype(_MXU_DTYPE)

    def seg(lo, width):
        return jnp.dot(xb, w_ref[:, lo:lo + width], preferred_element_type=jnp.float32)

    q_ref[...] = seg(_C_Q, ATTN_WIDTH).astype(q_ref.dtype)
    k_ref[...] = seg(_C_K, ATTN_WIDTH).astype(k_ref.dtype)
    v_ref[...] = seg(_C_VA, ATTN_WIDTH).astype(v_ref.dtype)
    qi_ref[...] = seg(_C_QI, IDX_HEADS * IDX_DIM).astype(qi_ref.dtype)
    kw_ref[...] = seg(_C_KW, LANES)
    gb_ref[...] = jax.nn.sigmoid(seg(_C_GB, D_MODEL))

    u_sc[...] = jax.nn.gelu(seg(_C_U, D_MODEL))
    vn = _layer_norm(jax.nn.gelu(seg(_C_V, D_MODEL)), lng_ref[...], lnb_ref[...])
    vn_sc[...] = vn.astype(vn_sc.dtype)
    row = lax.broadcasted_iota(jnp.int32, (CHUNK, CHUNK), 0)
    col = lax.broadcasted_iota(jnp.int32, (CHUNK, CHUNK), 1)
    causal = col <= row
    for g in range(SGU_GROUPS):
        wm = jnp.where(causal, sw_ref[g], 0.0).astype(_MXU_DTYPE)
        bias = sbt_ref[:, g:g + 1]
        cs = slice(g * SGU_GROUP_DIM, (g + 1) * SGU_GROUP_DIM)
        for c in range(tm // CHUNK):
            rs = slice(c * CHUNK, (c + 1) * CHUNK)
            s = jnp.dot(wm, vn_sc[rs, cs], preferred_element_type=jnp.float32) + bias
            a_sc[rs, cs] = (u_sc[rs, cs] * s).astype(a_sc.dtype)

    branch_a = jnp.dot(a_sc[...], wa_ref[...], preferred_element_type=jnp.float32)
    ma_ref[...] = jax.nn.sigmoid(seg(_C_GA, D_MODEL)) * branch_a


def _proj_call(x2, w_packed, ln_g, ln_b, sgu_w, sgu_bt, w_a):
    n = x2.shape[0]
    tm = _TM_PROJ
    row_spec = lambda width: pl.BlockSpec((tm, width), lambda i: (i, 0))
    out_shapes = (
        jax.ShapeDtypeStruct((n, ATTN_WIDTH), _MXU_DTYPE),
        jax.ShapeDtypeStruct((n, ATTN_WIDTH), _MXU_DTYPE),
        jax.ShapeDtypeStruct((n, ATTN_WIDTH), _MXU_DTYPE),
        jax.ShapeDtypeStruct((n, IDX_HEADS * IDX_DIM), _MXU_DTYPE),
        jax.ShapeDtypeStruct((n, LANES), jnp.float32),
        jax.ShapeDtypeStruct((n, D_MODEL), jnp.float32),
        jax.ShapeDtypeStruct((n, D_MODEL), jnp.float32),
    )
    return pl.pallas_call(
        _proj_kernel,
        grid=(n // tm,),
        in_specs=[
            row_spec(D_MODEL),
            _resident(w_packed.shape),
            _resident(ln_g.shape),
            _resident(ln_b.shape),
            _resident(sgu_w.shape),
            _resident(sgu_bt.shape),
            _resident(w_a.shape),
        ],
        out_specs=[row_spec(ATTN_WIDTH), row_spec(ATTN_WIDTH), row_spec(ATTN_WIDTH),
                   row_spec(IDX_HEADS * IDX_DIM), row_spec(LANES), row_spec(D_MODEL), row_spec(D_MODEL)],
        out_shape=out_shapes,
        scratch_shapes=[
            pltpu.VMEM((tm, D_MODEL), jnp.float32),
            pltpu.VMEM((tm, D_MODEL), _MXU_DTYPE),
            pltpu.VMEM((tm, D_MODEL), _MXU_DTYPE),
        ],
        compiler_params=pltpu.CompilerParams(
            dimension_semantics=("parallel",), vmem_limit_bytes=VMEM_LIMIT_BYTES),
        name="proj",
    )(x2, w_packed, ln_g, ln_b, sgu_w, sgu_bt, w_a)


def _key_to_f32(key):
    bits = jnp.where(key < 0, key & jnp.int32(0x7FFFFFFF), ~key)
    return lax.bitcast_convert_type(bits, jnp.float32)


def _index_kernel(qi_ref, kwq_ref, kwk_ref, mask_ref, sc_ref, *, top_k):
    tq = qi_ref.shape[0]
    seq = kwk_ref.shape[0]
    sk = min(_SK_IDX, seq)
    i = pl.program_id(1)
    row0 = i * tq
    nkb = (row0 + tq + sk - 1) // sk
    t_idx = row0 + lax.broadcasted_iota(jnp.int32, (tq, 1), 0)
    w8 = kwq_ref[:, IDX_DIM:IDX_DIM + IDX_HEADS] * (IDX_HEADS ** -0.5 * IDX_DIM ** -0.5)
    lane = lax.broadcasted_iota(jnp.int32, (sk, LANES), 1)
    nt_dims = (((1,), (1,)), ((), ()))

    def score_body(kb, carry):
        ks = pl.multiple_of(kb * sk, sk)
        kblk = jnp.where(lane < IDX_DIM, kwk_ref[pl.ds(ks, sk), :], 0.0)
        k_lo = kblk.astype(_MXU_DTYPE)
        k_hi = pltpu.roll(kblk, IDX_DIM, axis=1).astype(_MXU_DTYPE)
        acc = jnp.zeros((tq, sk), jnp.float32)
        for hp in range(IDX_HEADS // 2):
            qpair = qi_ref[:, hp * LANES:(hp + 1) * LANES]
            for half, kk in enumerate((k_lo, k_hi)):
                h = 2 * hp + half
                logits = lax.dot_general(qpair, kk, nt_dims, preferred_element_type=jnp.float32)
                acc = acc + w8[:, h:h + 1] * jnp.maximum(logits, 0.0)
        s_idx = ks + lax.broadcasted_iota(jnp.int32, (tq, sk), 1)
        sc_ref[:, pl.ds(ks, sk)] = jnp.where(s_idx <= t_idx, acc, -jnp.inf)
        return carry

    lax.fori_loop(0, nkb, score_body, 0)

    def count(indicator):
        def body(kb, acc):
            ks = pl.multiple_of(kb * sk, sk)
            for j in range(sk // LANES):
                off = pl.multiple_of(ks + j * LANES, LANES)
                acc = acc + indicator(sc_ref[:, pl.ds(off, LANES)], off)
            return acc
        acc = lax.fori_loop(0, nkb, body, jnp.zeros((tq, LANES), jnp.float32))
        return jnp.sum(acc, axis=1, keepdims=True)

    def bisect_body(it, key):
        trial = key | lax.shift_left(jnp.int32(1), 31 - it)
        cand = jnp.broadcast_to(_key_to_f32(trial), (tq, LANES))
        cnt = count(lambda tile, off: jnp.where(tile >= cand, 1.0, 0.0))
        return jnp.where(cnt >= top_k, trial, key)

    key = lax.fori_loop(0, 32, bisect_body, jnp.zeros((tq, 1), jnp.int32))
    tau = jnp.where(t_idx < top_k, -jnp.inf, _key_to_f32(key))
    tau_b = jnp.broadcast_to(tau, (tq, LANES))

    n_gt = count(lambda tile, off: jnp.where(tile > tau_b, 1.0, 0.0))
    n_ge = count(lambda tile, off: jnp.where(tile >= tau_b, 1.0, 0.0))
    need = top_k - n_gt
    excess_rows = jnp.where(t_idx >= top_k, jnp.where(n_ge > top_k, 1.0, 0.0), 0.0)
    has_excess = jnp.max(excess_rows) > 0.0
    lane_row = lax.broadcasted_iota(jnp.int32, (tq, LANES), 1)

    def tie_end_search():
        nbits = seq.bit_length()
        def body(it, end):
            trial = end | lax.shift_left(jnp.int32(1), nbits - 1 - it)
            trial_b = jnp.broadcast_to(trial, (tq, LANES))
            cnt = count(lambda tile, off: jnp.where(
                tile == tau_b, jnp.where(off + lane_row < trial_b, 1.0, 0.0), 0.0))
            return jnp.where(cnt <= need, trial, end)
        return lax.fori_loop(0, nbits, body, jnp.zeros((tq, 1), jnp.int32))

    tie_end = lax.cond(has_excess, tie_end_search, lambda: jnp.full((tq, 1), seq, jnp.int32))
    tie_end_b = jnp.broadcast_to(tie_end, (tq, LANES))
    t_b = jnp.broadcast_to(t_idx, (tq, LANES))

    def write_body(kb, carry):
        ks = pl.multiple_of(kb * sk, sk)
        for j in range(sk // LANES):
            off = pl.multiple_of(ks + j * LANES, LANES)
            tile = sc_ref[:, pl.ds(off, LANES)]
            s_b = off + lane_row
            tied = jnp.where(tile == tau_b, jnp.where(s_b < tie_end_b, 1, 0), 0)
            sel = jnp.where(tile > tau_b, 1, tied)
            sel = jnp.where(s_b <= t_b, sel, 0)
            mask_ref[:, pl.ds(off, LANES)] = sel.astype(mask_ref.dtype)
        return carry

    lax.fori_loop(0, nkb, write_body, 0)

    def zero_body(kb, carry):
        ks = pl.multiple_of(kb * sk, sk)
        mask_ref[:, pl.ds(ks, sk)] = jnp.zeros((tq, sk), mask_ref.dtype)
        return carry

    lax.fori_loop(nkb, seq // sk, zero_body, 0)


def _index_call(qidx, kw, bsz, seq, top_k):
    tq = min(_TQ_IDX, seq)
    nq = seq // tq
    return pl.pallas_call(
        functools.partial(_index_kernel, top_k=top_k),
        grid=(bsz, nq),
        in_specs=[
            pl.BlockSpec((tq, IDX_HEADS * IDX_DIM), lambda b, i: (b * nq + i, 0)),
            pl.BlockSpec((tq, LANES), lambda b, i: (b * nq + i, 0)),
            pl.BlockSpec((seq, LANES), lambda b, i: (b, 0)),
        ],
        out_specs=pl.BlockSpec((tq, seq), lambda b, i: (b * nq + i, 0)),
        out_shape=jax.ShapeDtypeStruct((bsz * seq, seq), jnp.int8),
        scratch_shapes=[pltpu.VMEM((tq, seq), jnp.float32)],
        compiler_params=pltpu.CompilerParams(
            dimension_semantics=("parallel", "parallel"), vmem_limit_bytes=VMEM_LIMIT_BYTES),
        name="index",
    )(qidx, kw, kw)


def _attn_kernel(q_ref, k_ref, v_ref, mask_ref, o_ref, m_sc, l_sc, acc_sc):
    tq = q_ref.shape[0]
    seq = k_ref.shape[0]
    sk = min(_SK_ATT, seq)
    i = pl.program_id(1)
    nkb = (i * tq + tq + sk - 1) // sk
    scale = ATTN_HEAD_DIM ** -0.5
    nt_dims = (((1,), (1,)), ((), ()))

    m_sc[...] = jnp.full(m_sc.shape, _NEG, jnp.float32)
    l_sc[...] = jnp.zeros(l_sc.shape, jnp.float32)
    acc_sc[...] = jnp.zeros(acc_sc.shape, jnp.float32)

    def body(kb, carry):
        ks = pl.multiple_of(kb * sk, sk)
        bias = (1.0 - mask_ref[:, pl.ds(ks, sk)].astype(jnp.float32)) * _NEG
        for h in range(ATTN_HEADS):
            hs = slice(h * ATTN_HEAD_DIM, (h + 1) * ATTN_HEAD_DIM)
            s = lax.dot_general(q_ref[:, hs], k_ref[pl.ds(ks, sk), hs], nt_dims,
                                preferred_element_type=jnp.float32) * scale + bias
            m_old = m_sc[h]
            m_new = jnp.maximum(m_old, jnp.max(s, axis=1, keepdims=True))
            p = jnp.exp(s - m_new)
            alpha = jnp.exp(m_old - m_new)
            l_sc[h] = alpha * l_sc[h] + jnp.sum(p, axis=1, keepdims=True)
            acc_sc[:, hs] = alpha * acc_sc[:, hs] + jnp.dot(
                p.astype(_MXU_DTYPE), v_ref[pl.ds(ks, sk), hs], preferred_element_type=jnp.float32)
            m_sc[h] = m_new
        return carry

    lax.fori_loop(0, nkb, body, 0)
    for h in range(ATTN_HEADS):
        hs = slice(h * ATTN_HEAD_DIM, (h + 1) * ATTN_HEAD_DIM)
        o_ref[:, hs] = (acc_sc[:, hs] / l_sc[h]).astype(o_ref.dtype)


def _attn_call(q, k, v, mask, bsz, seq):
    tq = min(_TQ_ATT, seq)
    nq = seq // tq
    return pl.pallas_call(
        _attn_kernel,
        grid=(bsz, nq),
        in_specs=[
            pl.BlockSpec((tq, ATTN_WIDTH), lambda b, i: (b * nq + i, 0)),
            pl.BlockSpec((seq, ATTN_WIDTH), lambda b, i: (b, 0)),
            pl.BlockSpec((seq, ATTN_WIDTH), lambda b, i: (b, 0)),
            pl.BlockSpec((tq, seq), lambda b, i: (b * nq + i, 0)),
        ],
        out_specs=pl.BlockSpec((tq, ATTN_WIDTH), lambda b, i: (b * nq + i, 0)),
        out_shape=jax.ShapeDtypeStruct((bsz * seq, ATTN_WIDTH), _MXU_DTYPE),
        scratch_shapes=[
            pltpu.VMEM((ATTN_HEADS, tq, 1), jnp.float32),
            pltpu.VMEM((ATTN_HEADS, tq, 1), jnp.float32),
            pltpu.VMEM((tq, ATTN_WIDTH), jnp.float32),
        ],
        compiler_params=pltpu.CompilerParams(
            dimension_semantics=("parallel", "parallel"), vmem_limit_bytes=VMEM_LIMIT_BYTES),
        name="attn",
    )(q, k, v, mask)


def _out_kernel(x_ref, ma_ref, gb_ref, b_ref, wb_ref, wo_ref, g1_ref, b1_ref,
                wup_ref, wdn_ref, g2_ref, b2_ref, o_ref, *, alpha):
    branch_b = jnp.dot(b_ref[...], wb_ref[...], preferred_element_type=jnp.float32)
    merged = ma_ref[...] + gb_ref[...] * branch_b
    y = alpha * x_ref[...] + jnp.dot(merged.astype(_MXU_DTYPE), wo_ref[...],
                                     preferred_element_type=jnp.float32)
    x1 = _layer_norm(y, g1_ref[...], b1_ref[...])
    x1b = x1.astype(_MXU_DTYPE)
    acc = jnp.zeros(x1.shape, jnp.float32)
    for c in range(FFN_DIM // _FFN_CHUNK):
        cs = slice(c * _FFN_CHUNK, (c + 1) * _FFN_CHUNK)
        h = jnp.square(jnp.maximum(jnp.dot(x1b, wup_ref[:, cs], preferred_element_type=jnp.float32), 0.0))
        acc = acc + jnp.dot(h.astype(_MXU_DTYPE), wdn_ref[cs, :], preferred_element_type=jnp.float32)
    o_ref[...] = _layer_norm(alpha * x1 + acc, g2_ref[...], b2_ref[...])


def _out_call(x2, ma, gb, battn, w_b, w_o, g1, b1, w_up, w_dn, g2, b2, alpha):
    n = x2.shape[0]
    tm = _TM_OUT
    row_spec = pl.BlockSpec((tm, D_MODEL), lambda i: (i, 0))
    return pl.pallas_call(
        functools.partial(_out_kernel, alpha=alpha),
        grid=(n // tm,),
        in_specs=[row_spec, row_spec, row_spec, row_spec,
                  _resident(w_b.shape), _resident(w_o.shape), _resident(g1.shape), _resident(b1.shape),
                  _resident(w_up.shape), _resident(w_dn.shape), _resident(g2.shape), _resident(b2.shape)],
        out_specs=row_spec,
        out_shape=jax.ShapeDtypeStruct((n, D_MODEL), jnp.float32),
        compiler_params=pltpu.CompilerParams(
            dimension_semantics=("parallel",), vmem_limit_bytes=VMEM_LIMIT_BYTES),
        name="out",
    )(x2, ma, gb, battn, w_b, w_o, g1, b1, w_up, w_dn, g2, b2)


def _pack_w_in(w_in):
    offs = [0]
    for width in IN_SPLITS:
        offs.append(offs[-1] + width)
    u, v, q, k, va, qi, ki, wi, ga, gb = (w_in[:, offs[j]:offs[j + 1]] for j in range(len(IN_SPLITS)))
    pad = jnp.zeros((w_in.shape[0], LANES - IDX_DIM - IDX_HEADS), w_in.dtype)
    return jnp.concatenate([u, v, q, k, va, ga, gb, qi, ki, wi, pad], axis=1).astype(_MXU_DTYPE)


def _layer(x, w_in, sgu_ln_g, sgu_ln_b, sgu_w, sgu_b, w_branch_a, w_branch_b, w_out,
           ln1_g, ln1_b, w_ffn_up, w_ffn_down, ln2_g, ln2_b, alpha):
    bsz, seq, _ = x.shape
    assert seq % CHUNK == 0 and (bsz * seq) % _TM_PROJ == 0 and (bsz * seq) % _TM_OUT == 0
    assert seq % min(_TQ_ATT, seq) == 0 and seq % min(_SK_ATT, seq) == 0 and seq % min(_SK_IDX, seq) == 0
    top_k = min(IDX_TOPK_MAX, seq // 4)
    x2 = x.reshape(bsz * seq, D_MODEL)
    vec = lambda a: a.reshape(1, -1)
    cast = lambda a: a.astype(_MXU_DTYPE)

    q, k, v, qidx, kw, ma, gb = _proj_call(
        x2, _pack_w_in(w_in), vec(sgu_ln_g), vec(sgu_ln_b), sgu_w, sgu_b.T, cast(w_branch_a))
    mask = _index_call(qidx, kw, bsz, seq, top_k)
    battn = _attn_call(q, k, v, mask, bsz, seq)
    out = _out_call(x2, ma, gb, battn, cast(w_branch_b), cast(w_out), vec(ln1_g), vec(ln1_b),
                    cast(w_ffn_up), cast(w_ffn_down), vec(ln2_g), vec(ln2_b), alpha)
    return out.reshape(bsz, seq, D_MODEL)


def kernel(x, w_in, sgu_ln_g, sgu_ln_b, sgu_w, sgu_b, w_branch_a, w_branch_b, w_out,
           ln1_g, ln1_b, w_ffn_up, w_ffn_down, ln2_g, ln2_b):
    depth = w_in.shape[0]
    alpha = (2.0 * depth) ** 0.25
    for i in range(depth):
        x = _layer(x, w_in[i], sgu_ln_g[i], sgu_ln_b[i], sgu_w[i], sgu_b[i], w_branch_a[i],
                   w_branch_b[i], w_out[i], ln1_g[i], ln1_b[i], w_ffn_up[i], w_ffn_down[i],
                   ln2_g[i], ln2_b[i], alpha)
    return x
```

```python
import functools

import jax
import jax.numpy as jnp
from jax import lax
from jax.experimental import pallas as pl
from jax.experimental.pallas import tpu as pltpu

D_MODEL = 1024
CHUNK = 128
SGU_GROUPS = 8
SGU_GROUP_DIM = D_MODEL // SGU_GROUPS
ATTN_HEADS = 8
ATTN_HEAD_DIM = 128
ATTN_WIDTH = ATTN_HEADS * ATTN_HEAD_DIM
IDX_HEADS = 8
IDX_DIM = 64
IDX_TOPK_MAX = 256
FFN_DIM = 4 * D_MODEL
LN_EPS = 1e-5
IN_SPLITS = (D_MODEL, D_MODEL, ATTN_WIDTH, ATTN_WIDTH, ATTN_WIDTH,
             IDX_HEADS * IDX_DIM, IDX_DIM, IDX_HEADS, D_MODEL, D_MODEL)

LANES = 128
VMEM_LIMIT_BYTES = 56 * 2**20

_MXU_DTYPE = jnp.bfloat16
_NEG = -1e30
_LOG2_E = 1.4426950408889634

_C_U, _C_V, _C_Q, _C_K, _C_GA, _C_GB = (i * D_MODEL for i in range(6))
_C_QI = 6 * D_MODEL
_C_KW = _C_QI + IDX_HEADS * IDX_DIM

_TM_PROJ = 256
_TQ_IDX = 256
_SK_IDX = 512
_ROWS_IDX = 32
_TQ_ATT = 256
_SK_ATT = 512
_TM_OUT = 256
_FFN_CHUNK = 1024

_NT_DIMS = (((1,), (1,)), ((), ()))


def _layer_norm(x, g, b):
    mu = jnp.mean(x, axis=-1, keepdims=True)
    xc = x - mu
    var = jnp.mean(xc * xc, axis=-1, keepdims=True)
    return xc * lax.rsqrt(var + LN_EPS) * g + b


def _resident(shape):
    nd = len(shape)
    return pl.BlockSpec(shape, lambda *_: (0,) * nd, pipeline_mode=pl.Buffered(1))


def _proj_kernel(x_ref, w_ref, wvt_ref, lng_ref, lnb_ref, sw_ref, sbt_ref, wa_ref,
                 q_ref, k_ref, vt_ref, qi_ref, kw_ref, ma_ref, gb_ref,
                 u_sc, vn_sc, a_sc):
    tm = x_ref.shape[0]
    xb = x_ref[...].astype(_MXU_DTYPE)

    def seg(lo, width):
        return jnp.dot(xb, w_ref[:, lo:lo + width], preferred_element_type=jnp.float32)

    q_ref[...] = seg(_C_Q, ATTN_WIDTH).astype(q_ref.dtype)
    k_ref[...] = seg(_C_K, ATTN_WIDTH).astype(k_ref.dtype)
    vt_ref[...] = lax.dot_general(wvt_ref[...], xb, _NT_DIMS,
                                  preferred_element_type=jnp.float32).astype(vt_ref.dtype)
    qi_ref[...] = seg(_C_QI, IDX_HEADS * IDX_DIM).astype(qi_ref.dtype)
    kw_ref[...] = seg(_C_KW, LANES)
    gb_ref[...] = jax.nn.sigmoid(seg(_C_GB, D_MODEL))

    u_sc[...] = jax.nn.gelu(seg(_C_U, D_MODEL))
    vn = _layer_norm(jax.nn.gelu(seg(_C_V, D_MODEL)), lng_ref[...], lnb_ref[...])
    vn_sc[...] = vn.astype(vn_sc.dtype)
    row = lax.broadcasted_iota(jnp.int32, (CHUNK, CHUNK), 0)
    col = lax.broadcasted_iota(jnp.int32, (CHUNK, CHUNK), 1)
    causal = col <= row
    for g in range(SGU_GROUPS):
        wm = jnp.where(causal, sw_ref[g], 0.0).astype(_MXU_DTYPE)
        bias = sbt_ref[:, g:g + 1]
        cs = slice(g * SGU_GROUP_DIM, (g + 1) * SGU_GROUP_DIM)
        for c in range(tm // CHUNK):
            rs = slice(c * CHUNK, (c + 1) * CHUNK)
            s = jnp.dot(wm, vn_sc[rs, cs], preferred_element_type=jnp.float32) + bias
            a_sc[rs, cs] = (u_sc[rs, cs] * s).astype(a_sc.dtype)

    branch_a = jnp.dot(a_sc[...], wa_ref[...], preferred_element_type=jnp.float32)
    ma_ref[...] = jax.nn.sigmoid(seg(_C_GA, D_MODEL)) * branch_a


def _proj_call(x2, w_packed, w_vt, ln_g, ln_b, sgu_w, sgu_bt, w_a, bsz, seq):
    n = x2.shape[0]
    tm = _TM_PROJ
    tiles_per_seq = seq // tm
    row_spec = lambda width: pl.BlockSpec((tm, width), lambda i: (i, 0))
    out_shapes = (
        jax.ShapeDtypeStruct((n, ATTN_WIDTH), _MXU_DTYPE),
        jax.ShapeDtypeStruct((n, ATTN_WIDTH), _MXU_DTYPE),
        jax.ShapeDtypeStruct((bsz * ATTN_WIDTH, seq), _MXU_DTYPE),
        jax.ShapeDtypeStruct((n, IDX_HEADS * IDX_DIM), _MXU_DTYPE),
        jax.ShapeDtypeStruct((n, LANES), jnp.float32),
        jax.ShapeDtypeStruct((n, D_MODEL), jnp.float32),
        jax.ShapeDtypeStruct((n, D_MODEL), jnp.float32),
    )
    vt_spec = pl.BlockSpec((ATTN_WIDTH, tm), lambda i: (i // tiles_per_seq, i % tiles_per_seq))
    return pl.pallas_call(
        _proj_kernel,
        grid=(n // tm,),
        in_specs=[
            row_spec(D_MODEL),
            _resident(w_packed.shape),
            _resident(w_vt.shape),
            _resident(ln_g.shape),
            _resident(ln_b.shape),
            _resident(sgu_w.shape),
            _resident(sgu_bt.shape),
            _resident(w_a.shape),
        ],
        out_specs=[row_spec(ATTN_WIDTH), row_spec(ATTN_WIDTH), vt_spec,
                   row_spec(IDX_HEADS * IDX_DIM), row_spec(LANES), row_spec(D_MODEL), row_spec(D_MODEL)],
        out_shape=out_shapes,
        scratch_shapes=[
            pltpu.VMEM((tm, D_MODEL), jnp.float32),
            pltpu.VMEM((tm, D_MODEL), _MXU_DTYPE),
            pltpu.VMEM((tm, D_MODEL), _MXU_DTYPE),
        ],
        compiler_params=pltpu.CompilerParams(
            dimension_semantics=("parallel",), vmem_limit_bytes=VMEM_LIMIT_BYTES),
        name="proj",
    )(x2, w_packed, w_vt, ln_g, ln_b, sgu_w, sgu_bt, w_a)


def _key_to_f32(key):
    bits = jnp.where(key < 0, key & jnp.int32(0x7FFFFFFF), ~key)
    return lax.bitcast_convert_type(bits, jnp.float32)


def _index_kernel(qi_ref, kwq_ref, kwk_ref, mask_ref, sc_ref, *, top_k):
    tq = qi_ref.shape[0]
    seq = kwk_ref.shape[0]
    sk = min(_SK_IDX, seq)
    rows = _ROWS_IDX
    i = pl.program_id(1)
    col0 = i * tq
    nkb = (col0 + tq + sk - 1) // sk
    t_row = col0 + lax.broadcasted_iota(jnp.int32, (1, tq), 1)
    w8 = jnp.transpose(kwq_ref[...])[IDX_DIM:IDX_DIM + IDX_HEADS, :] * (IDX_HEADS ** -0.5 * IDX_DIM ** -0.5)
    lane = lax.broadcasted_iota(jnp.int32, (sk, LANES), 1)

    def score_body(kb, carry):
        ks = pl.multiple_of(kb * sk, sk)
        kblk = jnp.where(lane < IDX_DIM, kwk_ref[pl.ds(ks, sk), :], 0.0)
        k_lo = kblk.astype(_MXU_DTYPE)
        k_hi = pltpu.roll(kblk, IDX_DIM, axis=1).astype(_MXU_DTYPE)
        blk = sc_ref.at[pl.ds(ks, sk), :]
        for hp in range(IDX_HEADS // 2):
            qpair = qi_ref[:, hp * LANES:(hp + 1) * LANES]
            for half, kk in enumerate((k_lo, k_hi)):
                h = 2 * hp + half
                logits = lax.dot_general(kk, qpair, _NT_DIMS, preferred_element_type=jnp.float32)
                term = w8[h:h + 1, :] * jnp.maximum(logits, 0.0)
                if h == 0:
                    blk[...] = term
                elif h < IDX_HEADS - 1:
                    blk[...] = blk[...] + term
                else:
                    s_col = ks + lax.broadcasted_iota(jnp.int32, (sk, tq), 0)
                    blk[...] = jnp.where(s_col <= t_row, blk[...] + term, -jnp.inf)
        return carry

    lax.fori_loop(0, nkb, score_body, 0)

    def count(indicator):
        def body(kb, acc):
            ks = pl.multiple_of(kb * sk, sk)
            for j in range(sk // rows):
                off = pl.multiple_of(ks + j * rows, rows)
                acc = acc + indicator(sc_ref[pl.ds(off, rows), :], off)
            return acc
        acc = lax.fori_loop(0, nkb, body, jnp.zeros((rows, tq), jnp.float32))
        return jnp.sum(acc, axis=0, keepdims=True)

    def bisect_body(it, key):
        trial = key | lax.shift_left(jnp.int32(1), 31 - it)
        cand = jnp.broadcast_to(_key_to_f32(trial), (rows, tq))
        cnt = count(lambda tile, off: jnp.where(tile >= cand, 1.0, 0.0))
        return jnp.where(cnt >= top_k, trial, key)

    key = lax.fori_loop(0, 32, bisect_body, jnp.zeros((1, tq), jnp.int32))
    tau = jnp.where(t_row < top_k, -jnp.inf, _key_to_f32(key))
    tau_b = jnp.broadcast_to(tau, (rows, tq))

    n_gt = count(lambda tile, off: jnp.where(tile > tau_b, 1.0, 0.0))
    n_ge = count(lambda tile, off: jnp.where(tile >= tau_b, 1.0, 0.0))
    need = top_k - n_gt
    excess = jnp.where(t_row >= top_k, jnp.where(n_ge > top_k, 1.0, 0.0), 0.0)
    has_excess = jnp.max(excess) > 0.0
    sub_row = lax.broadcasted_iota(jnp.int32, (rows, tq), 0)

    def tie_end_search():
        nbits = seq.bit_length()
        def body(it, end):
            trial = end | lax.shift_left(jnp.int32(1), nbits - 1 - it)
            trial_b = jnp.broadcast_to(trial, (rows, tq))
            cnt = count(lambda tile, off: jnp.where(
                tile == tau_b, jnp.where(off + sub_row < trial_b, 1.0, 0.0), 0.0))
            return jnp.where(cnt <= need, trial, end)
        return lax.fori_loop(0, nbits, body, jnp.zeros((1, tq), jnp.int32))

    tie_end = lax.cond(has_excess, tie_end_search, lambda: jnp.full((1, tq), seq, jnp.int32))
    tie_end_b = jnp.broadcast_to(tie_end, (rows, tq))
    t_b = jnp.broadcast_to(t_row, (rows, tq))

    def write_body(kb, carry):
        ks = pl.multiple_of(kb * sk, sk)
        for j in range(sk // rows):
            off = pl.multiple_of(ks + j * rows, rows)
            tile = sc_ref[pl.ds(off, rows), :]
            s_b = off + sub_row
            tied = jnp.where(tile == tau_b, jnp.where(s_b < tie_end_b, 1, 0), 0)
            sel = jnp.where(tile > tau_b, 1, tied)
            sel = jnp.where(s_b <= t_b, sel, 0)
            mask_ref[pl.ds(off, rows), :] = sel.astype(mask_ref.dtype)
        return carry

    lax.fori_loop(0, nkb, write_body, 0)

    def zero_body(kb, carry):
        ks = pl.multiple_of(kb * sk, sk)
        mask_ref[pl.ds(ks, sk), :] = jnp.zeros((sk, tq), mask_ref.dtype)
        return carry

    lax.fori_loop(nkb, seq // sk, zero_body, 0)


def _index_call(qidx, kw, bsz, seq, top_k):
    tq = min(_TQ_IDX, seq)
    nq = seq // tq
    return pl.pallas_call(
        functools.partial(_index_kernel, top_k=top_k),
        grid=(bsz, nq),
        in_specs=[
            pl.BlockSpec((tq, IDX_HEADS * IDX_DIM), lambda b, i: (b * nq + i, 0)),
            pl.BlockSpec((tq, LANES), lambda b, i: (b * nq + i, 0)),
            pl.BlockSpec((seq, LANES), lambda b, i: (b, 0)),
        ],
        out_specs=pl.BlockSpec((seq, tq), lambda b, i: (b, i)),
        out_shape=jax.ShapeDtypeStruct((bsz * seq, seq), jnp.int8),
        scratch_shapes=[pltpu.VMEM((seq, tq), jnp.float32)],
        compiler_params=pltpu.CompilerParams(
            dimension_semantics=("parallel", "parallel"), vmem_limit_bytes=VMEM_LIMIT_BYTES),
        name="index",
    )(qidx, kw, kw)


def _attn_kernel(q_ref, k_ref, vt_ref, mask_ref, o_ref, m_sc, l_sc, acc_sc, s_sc, bias_sc):
    tq = q_ref.shape[0]
    seq = k_ref.shape[0]
    sk = min(_SK_ATT, seq)
    i = pl.program_id(1)
    nkb = (i * tq + tq + sk - 1) // sk
    exp2_scale = ATTN_HEAD_DIM ** -0.5 * _LOG2_E

    m_sc[...] = jnp.full(m_sc.shape, _NEG, jnp.float32)
    l_sc[...] = jnp.zeros(l_sc.shape, jnp.float32)
    acc_sc[...] = jnp.zeros(acc_sc.shape, jnp.float32)

    def body(kb, carry):
        ks = pl.multiple_of(kb * sk, sk)
        bias_sc[...] = (1.0 - mask_ref[pl.ds(ks, sk), :].astype(jnp.float32)) * _NEG

        def logits(h):
            hs = slice(h * ATTN_HEAD_DIM, (h + 1) * ATTN_HEAD_DIM)
            s = lax.dot_general(k_ref[pl.ds(ks, sk), hs], q_ref[:, hs], _NT_DIMS,
                                preferred_element_type=jnp.float32) + bias_sc[...]
            s_sc[h % 2] = s
            return jnp.max(s, axis=0, keepdims=True)

        mx_next = logits(0)
        for h in range(ATTN_HEADS):
            hs = slice(h * ATTN_HEAD_DIM, (h + 1) * ATTN_HEAD_DIM)
            mx = mx_next
            if h + 1 < ATTN_HEADS:
                mx_next = logits(h + 1)
            m_old = m_sc[h]
            m_new = jnp.maximum(m_old, mx)
            p = jnp.exp2((s_sc[h % 2] - m_new) * exp2_scale)
            alpha = jnp.exp2((m_old - m_new) * exp2_scale)
            l_sc[h] = alpha * l_sc[h] + jnp.sum(p, axis=0, keepdims=True)
            acc_sc[hs, :] = alpha * acc_sc[hs, :] + jnp.dot(
                vt_ref[hs, pl.ds(ks, sk)], p.astype(_MXU_DTYPE), preferred_element_type=jnp.float32)
            m_sc[h] = m_new
        return carry

    lax.fori_loop(0, nkb, body, 0)
    for h in range(ATTN_HEADS):
        hs = slice(h * ATTN_HEAD_DIM, (h + 1) * ATTN_HEAD_DIM)
        o_ref[:, hs] = jnp.transpose(acc_sc[hs, :] / l_sc[h]).astype(o_ref.dtype)


def _attn_call(q, k, vt, mask, bsz, seq):
    tq = min(_TQ_ATT, seq)
    nq = seq // tq
    return pl.pallas_call(
        _attn_kernel,
        grid=(bsz, nq),
        in_specs=[
            pl.BlockSpec((tq, ATTN_WIDTH), lambda b, i: (b * nq + i, 0)),
            pl.BlockSpec((seq, ATTN_WIDTH), lambda b, i: (b, 0)),
            pl.BlockSpec((ATTN_WIDTH, seq), lambda b, i: (b, 0)),
            pl.BlockSpec((seq, tq), lambda b, i: (b, i)),
        ],
        out_specs=pl.BlockSpec((tq, ATTN_WIDTH), lambda b, i: (b * nq + i, 0)),
        out_shape=jax.ShapeDtypeStruct((bsz * seq, ATTN_WIDTH), _MXU_DTYPE),
        scratch_shapes=[
            pltpu.VMEM((ATTN_HEADS, 1, tq), jnp.float32),
            pltpu.VMEM((ATTN_HEADS, 1, tq), jnp.float32),
            pltpu.VMEM((ATTN_WIDTH, tq), jnp.float32),
            pltpu.VMEM((2, min(_SK_ATT, seq), tq), jnp.float32),
            pltpu.VMEM((min(_SK_ATT, seq), tq), jnp.float32),
        ],
        compiler_params=pltpu.CompilerParams(
            dimension_semantics=("parallel", "parallel"), vmem_limit_bytes=VMEM_LIMIT_BYTES),
        name="attn",
    )(q, k, vt, mask)


def _out_kernel(x_ref, ma_ref, gb_ref, b_ref, wb_ref, wo_ref, g1_ref, b1_ref,
                wup_ref, wdn_ref, g2_ref, b2_ref, o_ref, *, alpha):
    branch_b = jnp.dot(b_ref[...], wb_ref[...], preferred_element_type=jnp.float32)
    merged = ma_ref[...] + gb_ref[...] * branch_b
    y = alpha * x_ref[...] + jnp.dot(merged.astype(_MXU_DTYPE), wo_ref[...],
                                     preferred_element_type=jnp.float32)
    x1 = _layer_norm(y, g1_ref[...], b1_ref[...])
    x1b = x1.astype(_MXU_DTYPE)
    acc = jnp.zeros(x1.shape, jnp.float32)
    for c in range(FFN_DIM // _FFN_CHUNK):
        cs = slice(c * _FFN_CHUNK, (c + 1) * _FFN_CHUNK)
        h = jnp.square(jnp.maximum(jnp.dot(x1b, wup_ref[:, cs], preferred_element_type=jnp.float32), 0.0))
        acc = acc + jnp.dot(h.astype(_MXU_DTYPE), wdn_ref[cs, :], preferred_element_type=jnp.float32)
    o_ref[...] = _layer_norm(alpha * x1 + acc, g2_ref[...], b2_ref[...])


def _out_call(x2, ma, gb, battn, w_b, w_o, g1, b1, w_up, w_dn, g2, b2, alpha):
    n = x2.shape[0]
    tm = _TM_OUT
    row_spec = pl.BlockSpec((tm, D_MODEL), lambda i: (i, 0))
    return pl.pallas_call(
        functools.partial(_out_kernel, alpha=alpha),
        grid=(n // tm,),
        in_specs=[row_spec, row_spec, row_spec, row_spec,
                  _resident(w_b.shape), _resident(w_o.shape), _resident(g1.shape), _resident(b1.shape),
                  _resident(w_up.shape), _resident(w_dn.shape), _resident(g2.shape), _resident(b2.shape)],
        out_specs=row_spec,
        out_shape=jax.ShapeDtypeStruct((n, D_MODEL), jnp.float32),
        compiler_params=pltpu.CompilerParams(
            dimension_semantics=("parallel",), vmem_limit_bytes=VMEM_LIMIT_BYTES),
        name="out",
    )(x2, ma, gb, battn, w_b, w_o, g1, b1, w_up, w_dn, g2, b2)


def _pack_w_in(w_in):
    offs = [0]
    for width in IN_SPLITS:
        offs.append(offs[-1] + width)
    u, v, q, k, va, qi, ki, wi, ga, gb = (w_in[:, offs[j]:offs[j + 1]] for j in range(len(IN_SPLITS)))
    pad = jnp.zeros((w_in.shape[0], LANES - IDX_DIM - IDX_HEADS), w_in.dtype)
    packed = jnp.concatenate([u, v, q, k, ga, gb, qi, ki, wi, pad], axis=1).astype(_MXU_DTYPE)
    return packed, va.T.astype(_MXU_DTYPE)


def _layer(x, w_in, sgu_ln_g, sgu_ln_b, sgu_w, sgu_b, w_branch_a, w_branch_b, w_out,
           ln1_g, ln1_b, w_ffn_up, w_ffn_down, ln2_g, ln2_b, alpha):
    bsz, seq, _ = x.shape
    assert seq % _TM_PROJ == 0 and seq % _TM_OUT == 0 and _TM_PROJ % CHUNK == 0
    assert seq % min(_TQ_ATT, seq) == 0 and seq % min(_SK_ATT, seq) == 0
    assert seq % min(_TQ_IDX, seq) == 0 and seq % min(_SK_IDX, seq) == 0
    top_k = min(IDX_TOPK_MAX, seq // 4)
    x2 = x.reshape(bsz * seq, D_MODEL)
    vec = lambda a: a.reshape(1, -1)
    cast = lambda a: a.astype(_MXU_DTYPE)

    w_packed, w_vt = _pack_w_in(w_in)
    q, k, vt, qidx, kw, ma, gb = _proj_call(
        x2, w_packed, w_vt, vec(sgu_ln_g), vec(sgu_ln_b), sgu_w, sgu_b.T, cast(w_branch_a), bsz, seq)
    mask = _index_call(qidx, kw, bsz, seq, top_k)
    battn = _attn_call(q, k, vt, mask, bsz, seq)
    out = _out_call(x2, ma, gb, battn, cast(w_branch_b), cast(w_out), vec(ln1_g), vec(ln1_b),
                    cast(w_ffn_up), cast(w_ffn_down), vec(ln2_g), vec(ln2_b), alpha)
    return out.reshape(bsz, seq, D_MODEL)


def kernel(x, w_in, sgu_ln_g, sgu_ln_b, sgu_w, sgu_b, w_branch_a, w_branch_b, w_out,
           ln1_g, ln1_b, w_ffn_up, w_ffn_down, ln2_g, ln2_b):
    depth = w_in.shape[0]
    alpha = (2.0 * depth) ** 0.25
    for i in range(depth):
        x = _layer(x, w_in[i], sgu_ln_g[i], sgu_ln_b[i], sgu_w[i], sgu_b[i], w_branch_a[i],
                   w_branch_b[i], w_out[i], ln1_g[i], ln1_b[i], w_ffn_up[i], w_ffn_down[i],
                   ln2_g[i], ln2_b[i], alpha)
    return x
```

```python
import functools

import jax
import jax.numpy as jnp
from jax import lax
from jax.experimental import pallas as pl
from jax.experimental.pallas import tpu as pltpu

D_MODEL = 1024
CHUNK = 128
SGU_GROUPS = 8
SGU_GROUP_DIM = D_MODEL // SGU_GROUPS
ATTN_HEADS = 8
ATTN_HEAD_DIM = 128
ATTN_WIDTH = ATTN_HEADS * ATTN_HEAD_DIM
IDX_HEADS = 8
IDX_DIM = 64
IDX_TOPK_MAX = 256
FFN_DIM = 4 * D_MODEL
LN_EPS = 1e-5
IN_SPLITS = (D_MODEL, D_MODEL, ATTN_WIDTH, ATTN_WIDTH, ATTN_WIDTH,
             IDX_HEADS * IDX_DIM, IDX_DIM, IDX_HEADS, D_MODEL, D_MODEL)

LANES = 128
VMEM_LIMIT_BYTES = 56 * 2**20

_MXU_DTYPE = jnp.bfloat16
_NEG = -1e30
_LOG2_E = 1.4426950408889634

_C_U, _C_V, _C_Q, _C_K, _C_GA, _C_GB = (i * D_MODEL for i in range(6))
_C_QI = 6 * D_MODEL
_C_KW = _C_QI + IDX_HEADS * IDX_DIM

_TM_PROJ = 256
_TQ_IDX = 256
_SK_IDX = 512
_ROWS_IDX = 32
_TQ_ATT = 256
_SK_ATT = 512
_TM_OUT = 256
_FFN_CHUNK = 1024

_NT_DIMS = (((1,), (1,)), ((), ()))
_VT_ROWS = ATTN_HEAD_DIM + 16


def _layer_norm(x, g, b):
    mu = jnp.mean(x, axis=-1, keepdims=True)
    xc = x - mu
    var = jnp.mean(xc * xc, axis=-1, keepdims=True)
    return xc * lax.rsqrt(var + LN_EPS) * g + b


def _resident(shape):
    nd = len(shape)
    return pl.BlockSpec(shape, lambda *_: (0,) * nd, pipeline_mode=pl.Buffered(1))


def _proj_kernel(x_ref, w_ref, wvt_ref, lng_ref, lnb_ref, sw_ref, sbt_ref, wa_ref,
                 q_ref, k_ref, vt_ref, qi_ref, kw_ref, ma_ref, gb_ref,
                 u_sc, vn_sc, a_sc):
    tm = x_ref.shape[0]
    xb = x_ref[...].astype(_MXU_DTYPE)

    def seg(lo, width):
        return jnp.dot(xb, w_ref[:, lo:lo + width], preferred_element_type=jnp.float32)

    q_ref[...] = (seg(_C_Q, ATTN_WIDTH) * (ATTN_HEAD_DIM ** -0.5 * _LOG2_E)).astype(q_ref.dtype)
    k_ref[...] = seg(_C_K, ATTN_WIDTH).astype(k_ref.dtype)
    vt = lax.dot_general(wvt_ref[...], xb, _NT_DIMS, preferred_element_type=jnp.float32)
    for h in range(ATTN_HEADS):
        vt_ref[h, :ATTN_HEAD_DIM, :] = vt[h * ATTN_HEAD_DIM:(h + 1) * ATTN_HEAD_DIM, :].astype(vt_ref.dtype)
        vt_ref[h, ATTN_HEAD_DIM:, :] = jnp.ones((_VT_ROWS - ATTN_HEAD_DIM, tm), vt_ref.dtype)
    qi_ref[...] = seg(_C_QI, IDX_HEADS * IDX_DIM).astype(qi_ref.dtype)
    kw_ref[...] = seg(_C_KW, LANES)
    gb_ref[...] = jax.nn.sigmoid(seg(_C_GB, D_MODEL))

    u_sc[...] = jax.nn.gelu(seg(_C_U, D_MODEL))
    vn = _layer_norm(jax.nn.gelu(seg(_C_V, D_MODEL)), lng_ref[...], lnb_ref[...])
    vn_sc[...] = vn.astype(vn_sc.dtype)
    row = lax.broadcasted_iota(jnp.int32, (CHUNK, CHUNK), 0)
    col = lax.broadcasted_iota(jnp.int32, (CHUNK, CHUNK), 1)
    causal = col <= row
    for g in range(SGU_GROUPS):
        wm = jnp.where(causal, sw_ref[g], 0.0).astype(_MXU_DTYPE)
        bias = sbt_ref[:, g:g + 1]
        cs = slice(g * SGU_GROUP_DIM, (g + 1) * SGU_GROUP_DIM)
        for c in range(tm // CHUNK):
            rs = slice(c * CHUNK, (c + 1) * CHUNK)
            s = jnp.dot(wm, vn_sc[rs, cs], preferred_element_type=jnp.float32) + bias
            a_sc[rs, cs] = (u_sc[rs, cs] * s).astype(a_sc.dtype)

    branch_a = jnp.dot(a_sc[...], wa_ref[...], preferred_element_type=jnp.float32)
    ma_ref[...] = jax.nn.sigmoid(seg(_C_GA, D_MODEL)) * branch_a


def _proj_call(x2, w_packed, w_vt, ln_g, ln_b, sgu_w, sgu_bt, w_a, bsz, seq):
    n = x2.shape[0]
    tm = _TM_PROJ
    tiles_per_seq = seq // tm
    row_spec = lambda width: pl.BlockSpec((tm, width), lambda i: (i, 0))
    out_shapes = (
        jax.ShapeDtypeStruct((n, ATTN_WIDTH), _MXU_DTYPE),
        jax.ShapeDtypeStruct((n, ATTN_WIDTH), _MXU_DTYPE),
        jax.ShapeDtypeStruct((bsz * ATTN_HEADS, _VT_ROWS, seq), _MXU_DTYPE),
        jax.ShapeDtypeStruct((n, IDX_HEADS * IDX_DIM), _MXU_DTYPE),
        jax.ShapeDtypeStruct((n, LANES), jnp.float32),
        jax.ShapeDtypeStruct((n, D_MODEL), jnp.float32),
        jax.ShapeDtypeStruct((n, D_MODEL), jnp.float32),
    )
    vt_spec = pl.BlockSpec((ATTN_HEADS, _VT_ROWS, tm), lambda i: (i // tiles_per_seq, 0, i % tiles_per_seq))
    return pl.pallas_call(
        _proj_kernel,
        grid=(n // tm,),
        in_specs=[
            row_spec(D_MODEL),
            _resident(w_packed.shape),
            _resident(w_vt.shape),
            _resident(ln_g.shape),
            _resident(ln_b.shape),
            _resident(sgu_w.shape),
            _resident(sgu_bt.shape),
            _resident(w_a.shape),
        ],
        out_specs=[row_spec(ATTN_WIDTH), row_spec(ATTN_WIDTH), vt_spec,
                   row_spec(IDX_HEADS * IDX_DIM), row_spec(LANES), row_spec(D_MODEL), row_spec(D_MODEL)],
        out_shape=out_shapes,
        scratch_shapes=[
            pltpu.VMEM((tm, D_MODEL), jnp.float32),
            pltpu.VMEM((tm, D_MODEL), _MXU_DTYPE),
            pltpu.VMEM((tm, D_MODEL), _MXU_DTYPE),
        ],
        compiler_params=pltpu.CompilerParams(
            dimension_semantics=("parallel",), vmem_limit_bytes=VMEM_LIMIT_BYTES),
        name="proj",
    )(x2, w_packed, w_vt, ln_g, ln_b, sgu_w, sgu_bt, w_a)


def _key_to_f32(key):
    bits = jnp.where(key < 0, key & jnp.int32(0x7FFFFFFF), ~key)
    return lax.bitcast_convert_type(bits, jnp.float32)


_BISECT_STEPS = 32
_BISECT_ALWAYS = 20
_BISECT_GROUP = 4


def _index_kernel(qi_ref, kwq_ref, kwk_ref, mask_ref, sc_ref, *, top_k):
    tq = qi_ref.shape[0]
    seq = kwk_ref.shape[0]
    sk = min(_SK_IDX, seq)
    rows = _ROWS_IDX
    i = pl.program_id(1)
    col0 = i * tq
    nkb = (col0 + tq + sk - 1) // sk
    t_row = col0 + lax.broadcasted_iota(jnp.int32, (1, tq), 1)
    w8 = jnp.transpose(kwq_ref[...])[IDX_DIM:IDX_DIM + IDX_HEADS, :] * (IDX_HEADS ** -0.5 * IDX_DIM ** -0.5)
    lane = lax.broadcasted_iota(jnp.int32, (sk, LANES), 1)

    def score_body(kb, carry):
        ks = pl.multiple_of(kb * sk, sk)
        kblk = jnp.where(lane < IDX_DIM, kwk_ref[pl.ds(ks, sk), :], 0.0)
        k_lo = kblk.astype(_MXU_DTYPE)
        k_hi = pltpu.roll(kblk, IDX_DIM, axis=1).astype(_MXU_DTYPE)
        blk = sc_ref.at[pl.ds(ks, sk), :]
        for hp in range(IDX_HEADS // 2):
            qpair = qi_ref[:, hp * LANES:(hp + 1) * LANES]
            for half, kk in enumerate((k_lo, k_hi)):
                h = 2 * hp + half
                logits = lax.dot_general(kk, qpair, _NT_DIMS, preferred_element_type=jnp.float32)
                term = w8[h:h + 1, :] * jnp.maximum(logits, 0.0)
                if h == 0:
                    blk[...] = term
                elif h < IDX_HEADS - 1:
                    blk[...] = blk[...] + term
                else:
                    s_col = ks + lax.broadcasted_iota(jnp.int32, (sk, tq), 0)
                    blk[...] = jnp.where(s_col <= t_row, blk[...] + term, -jnp.inf)
        return carry

    lax.fori_loop(0, nkb, score_body, 0)

    def for_tiles(fn, init):
        def body(kb, carry):
            ks = pl.multiple_of(kb * sk, sk)
            for j in range(sk // rows):
                off = pl.multiple_of(ks + j * rows, rows)
                carry = fn(carry, sc_ref[pl.ds(off, rows), :], off)
            return carry
        return lax.fori_loop(0, nkb, body, init)

    def count(indicator):
        acc = for_tiles(lambda acc, tile, off: acc + indicator(tile, off), jnp.zeros((rows, tq), jnp.float32))
        return jnp.sum(acc, axis=0, keepdims=True)

    keep_all = t_row < top_k

    def bisect_step(it, carry):
        key, cnt_key = carry
        trial = key | lax.shift_left(jnp.int32(1), _BISECT_STEPS - 1 - it)
        cand = jnp.broadcast_to(_key_to_f32(trial), (rows, tq))
        cnt = count(lambda tile, off: jnp.where(tile >= cand, 1.0, 0.0))
        ok = cnt >= top_k
        return jnp.where(ok, trial, key), jnp.where(ok, cnt, cnt_key)

    state = (jnp.zeros((1, tq), jnp.int32), jnp.full((1, tq), float(seq), jnp.float32))
    state = lax.fori_loop(0, _BISECT_ALWAYS, bisect_step, state)
    for start in range(_BISECT_ALWAYS, _BISECT_STEPS, _BISECT_GROUP):
        unresolved = jnp.where(keep_all, 0.0, jnp.where(state[1] > top_k, 1.0, 0.0))
        state = lax.cond(
            jnp.max(unresolved) > 0.0,
            lambda st, start=start: lax.fori_loop(start, start + _BISECT_GROUP, bisect_step, st),
            lambda st: st,
            state)
    key, n_ge = state
    tau = jnp.where(keep_all, -jnp.inf, _key_to_f32(key))
    tau_b = jnp.broadcast_to(tau, (rows, tq))

    excess = jnp.where(keep_all, 0.0, jnp.where(n_ge > top_k, 1.0, 0.0))
    sub_row = lax.broadcasted_iota(jnp.int32, (rows, tq), 0)

    def tie_end_search():
        need = top_k - count(lambda tile, off: jnp.where(tile > tau_b, 1.0, 0.0))
        nbits = seq.bit_length()
        def body(it, end):
            trial = end | lax.shift_left(jnp.int32(1), nbits - 1 - it)
            trial_b = jnp.broadcast_to(trial, (rows, tq))
            cnt = count(lambda tile, off: jnp.where(
                tile == tau_b, jnp.where(off + sub_row < trial_b, 1.0, 0.0), 0.0))
            return jnp.where(cnt <= need, trial, end)
        return lax.fori_loop(0, nbits, body, jnp.zeros((1, tq), jnp.int32))

    t_b = jnp.broadcast_to(t_row, (rows, tq))

    def write_with_ties():
        tie_end_b = jnp.broadcast_to(tie_end_search(), (rows, tq))
        def write_mask(carry, tile, off):
            s_b = off + sub_row
            tied = jnp.where(tile == tau_b, jnp.where(s_b < tie_end_b, 1, 0), 0)
            sel = jnp.where(s_b <= t_b, jnp.where(tile > tau_b, 1, tied), 0)
            mask_ref[pl.ds(off, rows), :] = sel.astype(mask_ref.dtype)
            return carry
        for_tiles(write_mask, 0)

    def write_without_ties():
        def write_mask(carry, tile, off):
            sel = jnp.where(off + sub_row <= t_b, jnp.where(tile >= tau_b, 1, 0), 0)
            mask_ref[pl.ds(off, rows), :] = sel.astype(mask_ref.dtype)
            return carry
        for_tiles(write_mask, 0)

    lax.cond(jnp.max(excess) > 0.0, write_with_ties, write_without_ties)

    def zero_body(kb, carry):
        ks = pl.multiple_of(kb * sk, sk)
        mask_ref[pl.ds(ks, sk), :] = jnp.zeros((sk, tq), mask_ref.dtype)
        return carry

    lax.fori_loop(nkb, seq // sk, zero_body, 0)


def _index_call(qidx, kw, bsz, seq, top_k):
    tq = min(_TQ_IDX, seq)
    nq = seq // tq
    return pl.pallas_call(
        functools.partial(_index_kernel, top_k=top_k),
        grid=(bsz, nq),
        in_specs=[
            pl.BlockSpec((tq, IDX_HEADS * IDX_DIM), lambda b, i: (b * nq + i, 0)),
            pl.BlockSpec((tq, LANES), lambda b, i: (b * nq + i, 0)),
            pl.BlockSpec((seq, LANES), lambda b, i: (b, 0)),
        ],
        out_specs=pl.BlockSpec((seq, tq), lambda b, i: (b, i)),
        out_shape=jax.ShapeDtypeStruct((bsz * seq, seq), jnp.int8),
        scratch_shapes=[pltpu.VMEM((seq, tq), jnp.float32)],
        compiler_params=pltpu.CompilerParams(
            dimension_semantics=("parallel", "parallel"), vmem_limit_bytes=VMEM_LIMIT_BYTES),
        name="index",
    )(qidx, kw, kw)


def _attn_kernel(q_ref, k_ref, vt_ref, mask_ref, o_ref, m_sc, acc_sc, s_sc, bias_sc):
    tq = q_ref.shape[0]
    seq = k_ref.shape[0]
    sk = min(_SK_ATT, seq)
    i = pl.program_id(1)
    nkb = (i * tq + tq + sk - 1) // sk
    last_kb = seq // sk - 1

    m_sc[...] = jnp.full(m_sc.shape, _NEG, jnp.float32)
    acc_sc[...] = jnp.zeros(acc_sc.shape, jnp.float32)

    def stage_bias(kb):
        ks = pl.multiple_of(kb * sk, sk)
        bias_sc[kb % 2] = (1.0 - mask_ref[pl.ds(ks, sk), :].astype(jnp.float32)) * _NEG

    def logits(kb, h):
        ks = pl.multiple_of(kb * sk, sk)
        hs = slice(h * ATTN_HEAD_DIM, (h + 1) * ATTN_HEAD_DIM)
        half = sk // 2
        mx = None
        for part in range(2):
            rs = slice(part * half, (part + 1) * half)
            s = lax.dot_general(k_ref[pl.ds(ks + part * half, half), hs], q_ref[:, hs], _NT_DIMS,
                                preferred_element_type=jnp.float32) + bias_sc[kb % 2, rs, :]
            s_sc[h % 2, rs, :] = s
            part_max = jnp.max(s, axis=0, keepdims=True)
            mx = part_max if mx is None else jnp.maximum(mx, part_max)
        return mx

    def body(kb, mx_next):
        ks = pl.multiple_of(kb * sk, sk)
        kb_next = jnp.minimum(kb + 1, last_kb)
        stage_bias(kb_next)
        for h in range(ATTN_HEADS):
            mx = mx_next
            if h + 1 < ATTN_HEADS:
                mx_next = logits(kb, h + 1)
            else:
                mx_next = logits(kb_next, 0)
            m_old = m_sc[h]
            m_new = jnp.maximum(m_old, mx)
            alpha = jnp.exp2(m_old - m_new)
            half = sk // 2
            pv = None
            for part in range(2):
                rs = slice(part * half, (part + 1) * half)
                p = jnp.exp2(s_sc[h % 2, rs, :] - m_new).astype(_MXU_DTYPE)
                part_pv = jnp.dot(vt_ref[h, :, pl.ds(ks + part * half, half)], p,
                                  preferred_element_type=jnp.float32)
                pv = part_pv if pv is None else pv + part_pv
            acc_sc[h] = alpha * acc_sc[h] + pv
            m_sc[h] = m_new
        return mx_next

    stage_bias(0)
    lax.fori_loop(0, nkb, body, logits(0, 0))
    for h in range(ATTN_HEADS):
        hs = slice(h * ATTN_HEAD_DIM, (h + 1) * ATTN_HEAD_DIM)
        acc = acc_sc[h]
        out_t = acc[:ATTN_HEAD_DIM, :] / acc[ATTN_HEAD_DIM:ATTN_HEAD_DIM + 1, :]
        o_ref[:, hs] = jnp.transpose(out_t).astype(o_ref.dtype)


def _attn_call(q, k, vt, mask, bsz, seq):
    tq = min(_TQ_ATT, seq)
    nq = seq // tq
    return pl.pallas_call(
        _attn_kernel,
        grid=(bsz, nq),
        in_specs=[
            pl.BlockSpec((tq, ATTN_WIDTH), lambda b, i: (b * nq + i, 0)),
            pl.BlockSpec((seq, ATTN_WIDTH), lambda b, i: (b, 0)),
            pl.BlockSpec((ATTN_HEADS, _VT_ROWS, seq), lambda b, i: (b, 0, 0)),
            pl.BlockSpec((seq, tq), lambda b, i: (b, i)),
        ],
        out_specs=pl.BlockSpec((tq, ATTN_WIDTH), lambda b, i: (b * nq + i, 0)),
        out_shape=jax.ShapeDtypeStruct((bsz * seq, ATTN_WIDTH), _MXU_DTYPE),
        scratch_shapes=[
            pltpu.VMEM((ATTN_HEADS, 1, tq), jnp.float32),
            pltpu.VMEM((ATTN_HEADS, _VT_ROWS, tq), jnp.float32),
            pltpu.VMEM((2, min(_SK_ATT, seq), tq), jnp.float32),
            pltpu.VMEM((2, min(_SK_ATT, seq), tq), jnp.float32),
        ],
        compiler_params=pltpu.CompilerParams(
            dimension_semantics=("parallel", "parallel"), vmem_limit_bytes=VMEM_LIMIT_BYTES),
        name="attn",
    )(q, k, vt, mask)


def _out_kernel(x_ref, ma_ref, gb_ref, b_ref, wb_ref, wo_ref, g1_ref, b1_ref,
                wup_ref, wdn_ref, g2_ref, b2_ref, o_ref, *, alpha):
    branch_b = jnp.dot(b_ref[...], wb_ref[...], preferred_element_type=jnp.float32)
    merged = ma_ref[...] + gb_ref[...] * branch_b
    y = alpha * x_ref[...] + jnp.dot(merged.astype(_MXU_DTYPE), wo_ref[...],
                                     preferred_element_type=jnp.float32)
    x1 = _layer_norm(y, g1_ref[...], b1_ref[...])
    x1b = x1.astype(_MXU_DTYPE)
    acc = jnp.zeros(x1.shape, jnp.float32)
    for c in range(FFN_DIM // _FFN_CHUNK):
        cs = slice(c * _FFN_CHUNK, (c + 1) * _FFN_CHUNK)
        h = jnp.square(jnp.maximum(jnp.dot(x1b, wup_ref[:, cs], preferred_element_type=jnp.float32), 0.0))
        acc = acc + jnp.dot(h.astype(_MXU_DTYPE), wdn_ref[cs, :], preferred_element_type=jnp.float32)
    o_ref[...] = _layer_norm(alpha * x1 + acc, g2_ref[...], b2_ref[...])


def _out_call(x2, ma, gb, battn, w_b, w_o, g1, b1, w_up, w_dn, g2, b2, alpha):
    n = x2.shape[0]
    tm = _TM_OUT
    row_spec = pl.BlockSpec((tm, D_MODEL), lambda i: (i, 0))
    return pl.pallas_call(
        functools.partial(_out_kernel, alpha=alpha),
        grid=(n // tm,),
        in_specs=[row_spec, row_spec, row_spec, row_spec,
                  _resident(w_b.shape), _resident(w_o.shape), _resident(g1.shape), _resident(b1.shape),
                  _resident(w_up.shape), _resident(w_dn.shape), _resident(g2.shape), _resident(b2.shape)],
        out_specs=row_spec,
        out_shape=jax.ShapeDtypeStruct((n, D_MODEL), jnp.float32),
        compiler_params=pltpu.CompilerParams(
            dimension_semantics=("parallel",), vmem_limit_bytes=VMEM_LIMIT_BYTES),
        name="out",
    )(x2, ma, gb, battn, w_b, w_o, g1, b1, w_up, w_dn, g2, b2)


def _pack_w_in(w_in):
    offs = [0]
    for width in IN_SPLITS:
        offs.append(offs[-1] + width)
    u, v, q, k, va, qi, ki, wi, ga, gb = (w_in[:, offs[j]:offs[j + 1]] for j in range(len(IN_SPLITS)))
    pad = jnp.zeros((w_in.shape[0], LANES - IDX_DIM - IDX_HEADS), w_in.dtype)
    packed = jnp.concatenate([u, v, q, k, ga, gb, qi, ki, wi, pad], axis=1).astype(_MXU_DTYPE)
    return packed, va.T.astype(_MXU_DTYPE)


def _layer(x, w_in, sgu_ln_g, sgu_ln_b, sgu_w, sgu_b, w_branch_a, w_branch_b, w_out,
           ln1_g, ln1_b, w_ffn_up, w_ffn_down, ln2_g, ln2_b, alpha):
    bsz, seq, _ = x.shape
    assert seq % _TM_PROJ == 0 and seq % _TM_OUT == 0 and _TM_PROJ % CHUNK == 0
    assert seq % min(_TQ_ATT, seq) == 0 and seq % min(_SK_ATT, seq) == 0
    assert seq % min(_TQ_IDX, seq) == 0 and seq % min(_SK_IDX, seq) == 0
    top_k = min(IDX_TOPK_MAX, seq // 4)
    x2 = x.reshape(bsz * seq, D_MODEL)
    vec = lambda a: a.reshape(1, -1)
    cast = lambda a: a.astype(_MXU_DTYPE)

    w_packed, w_vt = _pack_w_in(w_in)
    q, k, vt, qidx, kw, ma, gb = _proj_call(
        x2, w_packed, w_vt, vec(sgu_ln_g), vec(sgu_ln_b), sgu_w, sgu_b.T, cast(w_branch_a), bsz, seq)
    mask = _index_call(qidx, kw, bsz, seq, top_k)
    battn = _attn_call(q, k, vt, mask, bsz, seq)
    out = _out_call(x2, ma, gb, battn, cast(w_branch_b), cast(w_out), vec(ln1_g), vec(ln1_b),
                    cast(w_ffn_up), cast(w_ffn_down), vec(ln2_g), vec(ln2_b), alpha)
    return out.reshape(bsz, seq, D_MODEL)


def kernel(x, w_in, sgu_ln_g, sgu_ln_b, sgu_w, sgu_b, w_branch_a, w_branch_b, w_out,
           ln1_g, ln1_b, w_ffn_up, w_ffn_down, ln2_g, ln2_b):
    depth = w_in.shape[0]
    alpha = (2.0 * depth) ** 0.25
    for i in range(depth):
        x = _layer(x, w_in[i], sgu_ln_g[i], sgu_ln_b[i], sgu_w[i], sgu_b[i], w_branch_a[i],
                   w_branch_b[i], w_out[i], ln1_g[i], ln1_b[i], w_ffn_up[i], w_ffn_down[i],
                   ln2_g[i], ln2_b[i], alpha)
    return x
```

```python
import functools

import jax
import jax.numpy as jnp
from jax import lax
from jax.experimental import pallas as pl
from jax.experimental.pallas import tpu as pltpu

D_MODEL = 1024
CHUNK = 128
SGU_GROUPS = 8
SGU_GROUP_DIM = D_MODEL // SGU_GROUPS
ATTN_HEADS = 8
ATTN_HEAD_DIM = 128
ATTN_WIDTH = ATTN_HEADS * ATTN_HEAD_DIM
IDX_HEADS = 8
IDX_DIM = 64
IDX_TOPK_MAX = 256
FFN_DIM = 4 * D_MODEL
LN_EPS = 1e-5
IN_SPLITS = (D_MODEL, D_MODEL, ATTN_WIDTH, ATTN_WIDTH, ATTN_WIDTH,
             IDX_HEADS * IDX_DIM, IDX_DIM, IDX_HEADS, D_MODEL, D_MODEL)

LANES = 128
VMEM_LIMIT_BYTES = 56 * 2**20

_MXU_DTYPE = jnp.bfloat16
_NEG = -1e30
_LOG2_E = 1.4426950408889634

_C_U, _C_V, _C_Q, _C_K, _C_GA, _C_GB = (i * D_MODEL for i in range(6))
_C_QI = 6 * D_MODEL
_C_KW = _C_QI + IDX_HEADS * IDX_DIM

_TM_PROJ = 512
_TQ_IDX = 512
_SK_IDX = 512
_ROWS_IDX = 32
_TQ_ATT = 512
_SK_ATT = 512
_TM_OUT = 512
_FFN_CHUNK = 1024

_NT_DIMS = (((1,), (1,)), ((), ()))
_VT_ROWS = ATTN_HEAD_DIM + 16


def _layer_norm(x, g, b):
    mu = jnp.mean(x, axis=-1, keepdims=True)
    xc = x - mu
    var = jnp.mean(xc * xc, axis=-1, keepdims=True)
    return xc * lax.rsqrt(var + LN_EPS) * g + b


def _resident(shape):
    nd = len(shape)
    return pl.BlockSpec(shape, lambda *_: (0,) * nd, pipeline_mode=pl.Buffered(1))


def _proj_kernel(x_ref, w_ref, wvt_ref, lng_ref, lnb_ref, sw_ref, sbt_ref, wa_ref,
                 q_ref, k_ref, vt_ref, qi_ref, kw_ref, ma_ref, gb_ref,
                 u_sc, vn_sc, a_sc):
    tm = x_ref.shape[0]
    xb = x_ref[...].astype(_MXU_DTYPE)

    def seg(lo, width):
        return jnp.dot(xb, w_ref[:, lo:lo + width], preferred_element_type=jnp.float32)

    q_ref[...] = (seg(_C_Q, ATTN_WIDTH) * (ATTN_HEAD_DIM ** -0.5 * _LOG2_E)).astype(q_ref.dtype)
    k_ref[...] = seg(_C_K, ATTN_WIDTH).astype(k_ref.dtype)
    vt = lax.dot_general(wvt_ref[...], xb, _NT_DIMS, preferred_element_type=jnp.float32)
    for h in range(ATTN_HEADS):
        vt_ref[h, :ATTN_HEAD_DIM, :] = vt[h * ATTN_HEAD_DIM:(h + 1) * ATTN_HEAD_DIM, :].astype(vt_ref.dtype)
        vt_ref[h, ATTN_HEAD_DIM:, :] = jnp.ones((_VT_ROWS - ATTN_HEAD_DIM, tm), vt_ref.dtype)
    qi_ref[...] = seg(_C_QI, IDX_HEADS * IDX_DIM).astype(qi_ref.dtype)
    kw_ref[...] = seg(_C_KW, LANES)
    gb_ref[...] = jax.nn.sigmoid(seg(_C_GB, D_MODEL))

    u_sc[...] = jax.nn.gelu(seg(_C_U, D_MODEL))
    vn = _layer_norm(jax.nn.gelu(seg(_C_V, D_MODEL)), lng_ref[...], lnb_ref[...])
    vn_sc[...] = vn.astype(vn_sc.dtype)
    row = lax.broadcasted_iota(jnp.int32, (CHUNK, CHUNK), 0)
    col = lax.broadcasted_iota(jnp.int32, (CHUNK, CHUNK), 1)
    causal = col <= row
    for g in range(SGU_GROUPS):
        wm = jnp.where(causal, sw_ref[g], 0.0).astype(_MXU_DTYPE)
        bias = sbt_ref[:, g:g + 1]
        cs = slice(g * SGU_GROUP_DIM, (g + 1) * SGU_GROUP_DIM)
        for c in range(tm // CHUNK):
            rs = slice(c * CHUNK, (c + 1) * CHUNK)
            s = jnp.dot(wm, vn_sc[rs, cs], preferred_element_type=jnp.float32) + bias
            a_sc[rs, cs] = (u_sc[rs, cs] * s).astype(a_sc.dtype)

    branch_a = jnp.dot(a_sc[...], wa_ref[...], preferred_element_type=jnp.float32)
    ma_ref[...] = jax.nn.sigmoid(seg(_C_GA, D_MODEL)) * branch_a


def _proj_call(x2, w_packed, w_vt, ln_g, ln_b, sgu_w, sgu_bt, w_a, bsz, seq):
    n = x2.shape[0]
    tm = _TM_PROJ
    tiles_per_seq = seq // tm
    row_spec = lambda width: pl.BlockSpec((tm, width), lambda i: (i, 0))
    out_shapes = (
        jax.ShapeDtypeStruct((n, ATTN_WIDTH), _MXU_DTYPE),
        jax.ShapeDtypeStruct((n, ATTN_WIDTH), _MXU_DTYPE),
        jax.ShapeDtypeStruct((bsz * ATTN_HEADS, _VT_ROWS, seq), _MXU_DTYPE),
        jax.ShapeDtypeStruct((n, IDX_HEADS * IDX_DIM), _MXU_DTYPE),
        jax.ShapeDtypeStruct((n, LANES), jnp.float32),
        jax.ShapeDtypeStruct((n, D_MODEL), jnp.float32),
        jax.ShapeDtypeStruct((n, D_MODEL), jnp.float32),
    )
    vt_spec = pl.BlockSpec((ATTN_HEADS, _VT_ROWS, tm), lambda i: (i // tiles_per_seq, 0, i % tiles_per_seq))
    return pl.pallas_call(
        _proj_kernel,
        grid=(n // tm,),
        in_specs=[
            row_spec(D_MODEL),
            _resident(w_packed.shape),
            _resident(w_vt.shape),
            _resident(ln_g.shape),
            _resident(ln_b.shape),
            _resident(sgu_w.shape),
            _resident(sgu_bt.shape),
            _resident(w_a.shape),
        ],
        out_specs=[row_spec(ATTN_WIDTH), row_spec(ATTN_WIDTH), vt_spec,
                   row_spec(IDX_HEADS * IDX_DIM), row_spec(LANES), row_spec(D_MODEL), row_spec(D_MODEL)],
        out_shape=out_shapes,
        scratch_shapes=[
            pltpu.VMEM((tm, D_MODEL), jnp.float32),
            pltpu.VMEM((tm, D_MODEL), _MXU_DTYPE),
            pltpu.VMEM((tm, D_MODEL), _MXU_DTYPE),
        ],
        compiler_params=pltpu.CompilerParams(
            dimension_semantics=("parallel",), vmem_limit_bytes=VMEM_LIMIT_BYTES),
        name="proj",
    )(x2, w_packed, w_vt, ln_g, ln_b, sgu_w, sgu_bt, w_a)


def _key_to_f32(key):
    bits = jnp.where(key < 0, key & jnp.int32(0x7FFFFFFF), ~key)
    return lax.bitcast_convert_type(bits, jnp.float32)


def _key16_to_f32(key16):
    pattern = jnp.where(key16 >= 32768, key16 - 32768, 65535 - key16)
    return lax.bitcast_convert_type(lax.shift_left(pattern, 16), jnp.float32)


_I32_MIN = -2 ** 31
_COARSE_DTYPE = jnp.bfloat16
_BF16_KEY_STEP = 1 << 16
_FINE_STEPS = 17
_FINE_ALWAYS = 5
_FINE_GROUP = 4


def _index_kernel(qi_ref, kwq_ref, kwk_ref, mask_ref, sc_ref, hb_ref, *, top_k):
    tq = qi_ref.shape[0]
    seq = kwk_ref.shape[0]
    sk = min(_SK_IDX, seq)
    rows = _ROWS_IDX
    i = pl.program_id(1)
    col0 = i * tq
    nkb = (col0 + tq + sk - 1) // sk
    t_row = col0 + lax.broadcasted_iota(jnp.int32, (1, tq), 1)
    w8 = jnp.transpose(kwq_ref[...])[IDX_DIM:IDX_DIM + IDX_HEADS, :] * (IDX_HEADS ** -0.5 * IDX_DIM ** -0.5)
    lane = lax.broadcasted_iota(jnp.int32, (sk, LANES), 1)

    def score_body(kb, carry):
        ks = pl.multiple_of(kb * sk, sk)
        kblk = jnp.where(lane < IDX_DIM, kwk_ref[pl.ds(ks, sk), :], 0.0)
        k_lo = kblk.astype(_MXU_DTYPE)
        k_hi = pltpu.roll(kblk, IDX_DIM, axis=1).astype(_MXU_DTYPE)
        blk = sc_ref.at[pl.ds(ks, sk), :]
        for hp in range(IDX_HEADS // 2):
            qpair = qi_ref[:, hp * LANES:(hp + 1) * LANES]
            for half, kk in enumerate((k_lo, k_hi)):
                h = 2 * hp + half
                logits = lax.dot_general(kk, qpair, _NT_DIMS, preferred_element_type=jnp.float32)
                term = w8[h:h + 1, :] * jnp.maximum(logits, 0.0)
                if h == 0:
                    blk[...] = term
                elif h < IDX_HEADS - 1:
                    blk[...] = blk[...] + term
                else:
                    s_col = ks + lax.broadcasted_iota(jnp.int32, (sk, tq), 0)
                    score = jnp.where(s_col <= t_row, blk[...] + term, -jnp.inf)
                    blk[...] = score
                    hb_ref[pl.ds(ks, sk), :] = score.astype(_COARSE_DTYPE)
        return carry

    lax.fori_loop(0, nkb, score_body, 0)

    def for_offsets(fn, init):
        def body(kb, carry):
            ks = pl.multiple_of(kb * sk, sk)
            for j in range(sk // rows):
                carry = fn(carry, pl.multiple_of(ks + j * rows, rows))
            return carry
        return lax.fori_loop(0, nkb, body, init)

    def for_tiles(fn, init):
        return for_offsets(lambda carry, off: fn(carry, sc_ref[pl.ds(off, rows), :], off), init)

    def count(indicator):
        acc = for_tiles(lambda acc, tile, off: acc + indicator(tile, off), jnp.zeros((rows, tq), jnp.float32))
        return jnp.sum(acc, axis=0, keepdims=True)

    keep_all = t_row < top_k

    def coarse_step(it, key16):
        trial = key16 | lax.shift_left(jnp.int32(1), 15 - it)
        cand = jnp.broadcast_to(_key16_to_f32(trial), (rows, tq)).astype(_COARSE_DTYPE)
        def add_tile(acc, off):
            ge = hb_ref[pl.ds(off, rows), :] >= cand
            return acc + jnp.where(ge, _COARSE_DTYPE(1), _COARSE_DTYPE(0))
        acc = for_offsets(add_tile, jnp.zeros((rows, tq), _COARSE_DTYPE))
        cnt = jnp.sum(acc.astype(jnp.float32), axis=0, keepdims=True)
        return jnp.where(cnt >= top_k, trial, key16)

    key16 = lax.fori_loop(0, 16, coarse_step, jnp.zeros((1, tq), jnp.int32))
    t16_bits = lax.bitcast_convert_type(_key16_to_f32(key16), jnp.int32)
    t16_key = jnp.where(t16_bits < 0, ~t16_bits, t16_bits | jnp.int32(_I32_MIN))
    key_base = t16_key - _BF16_KEY_STEP

    def fine_step(it, carry):
        offset, cnt_key = carry
        trial = offset | lax.shift_left(jnp.int32(1), _FINE_STEPS - 1 - it)
        cand = jnp.broadcast_to(_key_to_f32(key_base + trial), (rows, tq))
        cnt = count(lambda tile, off: jnp.where(tile >= cand, 1.0, 0.0))
        ok = cnt >= top_k
        return jnp.where(ok, trial, offset), jnp.where(ok, cnt, cnt_key)

    state = (jnp.zeros((1, tq), jnp.int32), jnp.full((1, tq), float(seq), jnp.float32))
    state = lax.fori_loop(0, _FINE_ALWAYS, fine_step, state)
    for start in range(_FINE_ALWAYS, _FINE_STEPS, _FINE_GROUP):
        stop = min(start + _FINE_GROUP, _FINE_STEPS)
        unresolved = jnp.where(keep_all, 0.0, jnp.where(state[1] > top_k, 1.0, 0.0))
        state = lax.cond(
            jnp.max(unresolved) > 0.0,
            lambda st, start=start, stop=stop: lax.fori_loop(start, stop, fine_step, st),
            lambda st: st,
            state)
    offset, n_ge = state
    tau = jnp.where(keep_all, -jnp.inf, _key_to_f32(key_base + offset))
    tau_b = jnp.broadcast_to(tau, (rows, tq))

    excess = jnp.where(keep_all, 0.0, jnp.where(n_ge > top_k, 1.0, 0.0))
    sub_row = lax.broadcasted_iota(jnp.int32, (rows, tq), 0)

    def tie_end_search():
        need = top_k - count(lambda tile, off: jnp.where(tile > tau_b, 1.0, 0.0))
        nbits = seq.bit_length()
        def body(it, end):
            trial = end | lax.shift_left(jnp.int32(1), nbits - 1 - it)
            trial_b = jnp.broadcast_to(trial, (rows, tq))
            cnt = count(lambda tile, off: jnp.where(
                tile == tau_b, jnp.where(off + sub_row < trial_b, 1.0, 0.0), 0.0))
            return jnp.where(cnt <= need, trial, end)
        return lax.fori_loop(0, nbits, body, jnp.zeros((1, tq), jnp.int32))

    t_b = jnp.broadcast_to(t_row, (rows, tq))

    def write_with_ties():
        tie_end_b = jnp.broadcast_to(tie_end_search(), (rows, tq))
        def write_mask(carry, tile, off):
            s_b = off + sub_row
            tied = jnp.where(tile == tau_b, jnp.where(s_b < tie_end_b, 1, 0), 0)
            sel = jnp.where(s_b <= t_b, jnp.where(tile > tau_b, 1, tied), 0)
            mask_ref[pl.ds(off, rows), :] = sel.astype(mask_ref.dtype)
            return carry
        for_tiles(write_mask, 0)

    def write_without_ties():
        def write_mask(carry, tile, off):
            sel = jnp.where(off + sub_row <= t_b, jnp.where(tile >= tau_b, 1, 0), 0)
            mask_ref[pl.ds(off, rows), :] = sel.astype(mask_ref.dtype)
            return carry
        for_tiles(write_mask, 0)

    lax.cond(jnp.max(excess) > 0.0, write_with_ties, write_without_ties)

    def zero_body(kb, carry):
        ks = pl.multiple_of(kb * sk, sk)
        mask_ref[pl.ds(ks, sk), :] = jnp.zeros((sk, tq), mask_ref.dtype)
        return carry

    lax.fori_loop(nkb, seq // sk, zero_body, 0)


def _index_call(qidx, kw, bsz, seq, top_k):
    tq = min(_TQ_IDX, seq)
    nq = seq // tq
    return pl.pallas_call(
        functools.partial(_index_kernel, top_k=top_k),
        grid=(bsz, nq),
        in_specs=[
            pl.BlockSpec((tq, IDX_HEADS * IDX_DIM), lambda b, i: (b * nq + i, 0)),
            pl.BlockSpec((tq, LANES), lambda b, i: (b * nq + i, 0)),
            pl.BlockSpec((seq, LANES), lambda b, i: (b, 0)),
        ],
        out_specs=pl.BlockSpec((seq, tq), lambda b, i: (b, i)),
        out_shape=jax.ShapeDtypeStruct((bsz * seq, seq), jnp.int8),
        scratch_shapes=[
            pltpu.VMEM((seq, tq), jnp.float32),
            pltpu.VMEM((seq, tq), _COARSE_DTYPE),
        ],
        compiler_params=pltpu.CompilerParams(
            dimension_semantics=("parallel", "parallel"), vmem_limit_bytes=VMEM_LIMIT_BYTES),
        name="index",
    )(qidx, kw, kw)


def _attn_kernel(q_ref, k_ref, vt_ref, mask_ref, o_ref, m_sc, acc_sc, s_sc, bias_sc):
    tq = q_ref.shape[0]
    seq = k_ref.shape[0]
    sk = min(_SK_ATT, seq)
    i = pl.program_id(1)
    nkb = (i * tq + tq + sk - 1) // sk
    last_kb = seq // sk - 1

    m_sc[...] = jnp.full(m_sc.shape, _NEG, jnp.float32)
    acc_sc[...] = jnp.zeros(acc_sc.shape, jnp.float32)

    def stage_bias(kb):
        ks = pl.multiple_of(kb * sk, sk)
        bias_sc[kb % 2] = (1.0 - mask_ref[pl.ds(ks, sk), :].astype(jnp.float32)) * _NEG

    def logits(kb, h):
        ks = pl.multiple_of(kb * sk, sk)
        hs = slice(h * ATTN_HEAD_DIM, (h + 1) * ATTN_HEAD_DIM)
        half = sk // 2
        mx = None
        for part in range(2):
            rs = slice(part * half, (part + 1) * half)
            s = lax.dot_general(k_ref[pl.ds(ks + part * half, half), hs], q_ref[:, hs], _NT_DIMS,
                                preferred_element_type=jnp.float32) + bias_sc[kb % 2, rs, :]
            s_sc[h % 2, rs, :] = s
            part_max = jnp.max(s, axis=0, keepdims=True)
            mx = part_max if mx is None else jnp.maximum(mx, part_max)
        return mx

    def body(kb, mx_next):
        ks = pl.multiple_of(kb * sk, sk)
        kb_next = jnp.minimum(kb + 1, last_kb)
        stage_bias(kb_next)
        for h in range(ATTN_HEADS):
            mx = mx_next
            if h + 1 < ATTN_HEADS:
                mx_next = logits(kb, h + 1)
            else:
                mx_next = logits(kb_next, 0)
            m_old = m_sc[h]
            m_new = jnp.maximum(m_old, mx)
            alpha = jnp.exp2(m_old - m_new)
            half = sk // 2
            pv = None
            for part in range(2):
                rs = slice(part * half, (part + 1) * half)
                p = jnp.exp2(s_sc[h % 2, rs, :] - m_new).astype(_MXU_DTYPE)
                part_pv = jnp.dot(vt_ref[h, :, pl.ds(ks + part * half, half)], p,
                                  preferred_element_type=jnp.float32)
                pv = part_pv if pv is None else pv + part_pv
            acc_sc[h] = alpha * acc_sc[h] + pv
            m_sc[h] = m_new
        return mx_next

    stage_bias(0)
    lax.fori_loop(0, nkb, body, logits(0, 0))
    for h in range(ATTN_HEADS):
        hs = slice(h * ATTN_HEAD_DIM, (h + 1) * ATTN_HEAD_DIM)
        acc = acc_sc[h]
        out_t = acc[:ATTN_HEAD_DIM, :] / acc[ATTN_HEAD_DIM:ATTN_HEAD_DIM + 1, :]
        o_ref[:, hs] = jnp.transpose(out_t).astype(o_ref.dtype)


def _attn_call(q, k, vt, mask, bsz, seq):
    tq = min(_TQ_ATT, seq)
    nq = seq // tq
    return pl.pallas_call(
        _attn_kernel,
        grid=(bsz, nq),
        in_specs=[
            pl.BlockSpec((tq, ATTN_WIDTH), lambda b, i: (b * nq + i, 0)),
            pl.BlockSpec((seq, ATTN_WIDTH), lambda b, i: (b, 0)),
            pl.BlockSpec((ATTN_HEADS, _VT_ROWS, seq), lambda b, i: (b, 0, 0)),
            pl.BlockSpec((seq, tq), lambda b, i: (b, i)),
        ],
        out_specs=pl.BlockSpec((tq, ATTN_WIDTH), lambda b, i: (b * nq + i, 0)),
        out_shape=jax.ShapeDtypeStruct((bsz * seq, ATTN_WIDTH), _MXU_DTYPE),
        scratch_shapes=[
            pltpu.VMEM((ATTN_HEADS, 1, tq), jnp.float32),
            pltpu.VMEM((ATTN_HEADS, _VT_ROWS, tq), jnp.float32),
            pltpu.VMEM((2, min(_SK_ATT, seq), tq), jnp.float32),
            pltpu.VMEM((2, min(_SK_ATT, seq), tq), jnp.float32),
        ],
        compiler_params=pltpu.CompilerParams(
            dimension_semantics=("parallel", "parallel"), vmem_limit_bytes=VMEM_LIMIT_BYTES),
        name="attn",
    )(q, k, vt, mask)


def _out_kernel(x_ref, ma_ref, gb_ref, b_ref, wb_ref, wo_ref, g1_ref, b1_ref,
                wup_ref, wdn_ref, g2_ref, b2_ref, o_ref, *, alpha):
    branch_b = jnp.dot(b_ref[...], wb_ref[...], preferred_element_type=jnp.float32)
    merged = ma_ref[...] + gb_ref[...] * branch_b
    y = alpha * x_ref[...] + jnp.dot(merged.astype(_MXU_DTYPE), wo_ref[...],
                                     preferred_element_type=jnp.float32)
    x1 = _layer_norm(y, g1_ref[...], b1_ref[...])
    x1b = x1.astype(_MXU_DTYPE)
    acc = jnp.zeros(x1.shape, jnp.float32)
    for c in range(FFN_DIM // _FFN_CHUNK):
        cs = slice(c * _FFN_CHUNK, (c + 1) * _FFN_CHUNK)
        h = jnp.square(jnp.maximum(jnp.dot(x1b, wup_ref[:, cs], preferred_element_type=jnp.float32), 0.0))
        acc = acc + jnp.dot(h.astype(_MXU_DTYPE), wdn_ref[cs, :], preferred_element_type=jnp.float32)
    o_ref[...] = _layer_norm(alpha * x1 + acc, g2_ref[...], b2_ref[...])


def _out_call(x2, ma, gb, battn, w_b, w_o, g1, b1, w_up, w_dn, g2, b2, alpha):
    n = x2.shape[0]
    tm = _TM_OUT
    row_spec = pl.BlockSpec((tm, D_MODEL), lambda i: (i, 0))
    return pl.pallas_call(
        functools.partial(_out_kernel, alpha=alpha),
        grid=(n // tm,),
        in_specs=[row_spec, row_spec, row_spec, row_spec,
                  _resident(w_b.shape), _resident(w_o.shape), _resident(g1.shape), _resident(b1.shape),
                  _resident(w_up.shape), _resident(w_dn.shape), _resident(g2.shape), _resident(b2.shape)],
        out_specs=row_spec,
        out_shape=jax.ShapeDtypeStruct((n, D_MODEL), jnp.float32),
        compiler_params=pltpu.CompilerParams(
            dimension_semantics=("parallel",), vmem_limit_bytes=VMEM_LIMIT_BYTES),
        name="out",
    )(x2, ma, gb, battn, w_b, w_o, g1, b1, w_up, w_dn, g2, b2)


def _pack_w_in(w_in):
    offs = [0]
    for width in IN_SPLITS:
        offs.append(offs[-1] + width)
    u, v, q, k, va, qi, ki, wi, ga, gb = (w_in[:, offs[j]:offs[j + 1]] for j in range(len(IN_SPLITS)))
    pad = jnp.zeros((w_in.shape[0], LANES - IDX_DIM - IDX_HEADS), w_in.dtype)
    packed = jnp.concatenate([u, v, q, k, ga, gb, qi, ki, wi, pad], axis=1).astype(_MXU_DTYPE)
    return packed, va.T.astype(_MXU_DTYPE)


def _layer(x, w_in, sgu_ln_g, sgu_ln_b, sgu_w, sgu_b, w_branch_a, w_branch_b, w_out,
           ln1_g, ln1_b, w_ffn_up, w_ffn_down, ln2_g, ln2_b, alpha):
    bsz, seq, _ = x.shape
    assert seq % _TM_PROJ == 0 and seq % _TM_OUT == 0 and _TM_PROJ % CHUNK == 0
    assert seq % min(_TQ_ATT, seq) == 0 and seq % min(_SK_ATT, seq) == 0
    assert seq % min(_TQ_IDX, seq) == 0 and seq % min(_SK_IDX, seq) == 0
    top_k = min(IDX_TOPK_MAX, seq // 4)
    x2 = x.reshape(bsz * seq, D_MODEL)
    vec = lambda a: a.reshape(1, -1)
    cast = lambda a: a.astype(_MXU_DTYPE)

    w_packed, w_vt = _pack_w_in(w_in)
    q, k, vt, qidx, kw, ma, gb = _proj_call(
        x2, w_packed, w_vt, vec(sgu_ln_g), vec(sgu_ln_b), sgu_w, sgu_b.T, cast(w_branch_a), bsz, seq)
    mask = _index_call(qidx, kw, bsz, seq, top_k)
    battn = _attn_call(q, k, vt, mask, bsz, seq)
    out = _out_call(x2, ma, gb, battn, cast(w_branch_b), cast(w_out), vec(ln1_g), vec(ln1_b),
                    cast(w_ffn_up), cast(w_ffn_down), vec(ln2_g), vec(ln2_b), alpha)
    return out.reshape(bsz, seq, D_MODEL)


def kernel(x, w_in, sgu_ln_g, sgu_ln_b, sgu_w, sgu_b, w_branch_a, w_branch_b, w_out,
           ln1_g, ln1_b, w_ffn_up, w_ffn_down, ln2_g, ln2_b):
    depth = w_in.shape[0]
    alpha = (2.0 * depth) ** 0.25
    for i in range(depth):
        x = _layer(x, w_in[i], sgu_ln_g[i], sgu_ln_b[i], sgu_w[i], sgu_b[i], w_branch_a[i],
                   w_branch_b[i], w_out[i], ln1_g[i], ln1_b[i], w_ffn_up[i], w_ffn_down[i],
                   ln2_g[i], ln2_b[i], alpha)
    return x
```

```python
import functools

import jax
import jax.numpy as jnp
from jax import lax
from jax.experimental import pallas as pl
from jax.experimental.pallas import tpu as pltpu

D_MODEL = 1024
CHUNK = 128
SGU_GROUPS = 8
SGU_GROUP_DIM = D_MODEL // SGU_GROUPS
ATTN_HEADS = 8
ATTN_HEAD_DIM = 128
ATTN_WIDTH = ATTN_HEADS * ATTN_HEAD_DIM
IDX_HEADS = 8
IDX_DIM = 64
IDX_TOPK_MAX = 256
FFN_DIM = 4 * D_MODEL
LN_EPS = 1e-5
IN_SPLITS = (D_MODEL, D_MODEL, ATTN_WIDTH, ATTN_WIDTH, ATTN_WIDTH,
             IDX_HEADS * IDX_DIM, IDX_DIM, IDX_HEADS, D_MODEL, D_MODEL)

LANES = 128
VMEM_LIMIT_BYTES = 56 * 2**20

_MXU_DTYPE = jnp.bfloat16
_NEG = -1e30
_LOG2_E = 1.4426950408889634

_C_U, _C_V, _C_Q, _C_K, _C_GA, _C_GB = (i * D_MODEL for i in range(6))
_C_QI = 6 * D_MODEL
_C_KW = _C_QI + IDX_HEADS * IDX_DIM

_TM_PROJ = 512
_TQ_IDX = 512
_SK_IDX = 512
_ROWS_IDX = 32
_TQ_ATT = 512
_SK_ATT = 512
_TM_OUT = 512
_FFN_CHUNK = 1024

_NT_DIMS = (((1,), (1,)), ((), ()))
_VT_ROWS = ATTN_HEAD_DIM + 16


def _layer_norm(x, g, b):
    mu = jnp.mean(x, axis=-1, keepdims=True)
    xc = x - mu
    var = jnp.mean(xc * xc, axis=-1, keepdims=True)
    return xc * lax.rsqrt(var + LN_EPS) * g + b


def _resident(shape):
    nd = len(shape)
    return pl.BlockSpec(shape, lambda *_: (0,) * nd, pipeline_mode=pl.Buffered(1))


def _proj_kernel(x_ref, w_ref, wvt_ref, lng_ref, lnb_ref, sw_ref, sbt_ref, wa_ref,
                 q_ref, k_ref, vt_ref, qi_ref, kw_ref, ma_ref, gb_ref,
                 u_sc, vn_sc, a_sc):
    tm = x_ref.shape[0]
    xb = x_ref[...].astype(_MXU_DTYPE)

    def seg(lo, width):
        return jnp.dot(xb, w_ref[:, lo:lo + width], preferred_element_type=jnp.float32)

    q_ref[...] = (seg(_C_Q, ATTN_WIDTH) * (ATTN_HEAD_DIM ** -0.5 * _LOG2_E)).astype(q_ref.dtype)
    k_ref[...] = seg(_C_K, ATTN_WIDTH).astype(k_ref.dtype)
    vt = lax.dot_general(wvt_ref[...], xb, _NT_DIMS, preferred_element_type=jnp.float32)
    for h in range(ATTN_HEADS):
        vt_ref[h, :ATTN_HEAD_DIM, :] = vt[h * ATTN_HEAD_DIM:(h + 1) * ATTN_HEAD_DIM, :].astype(vt_ref.dtype)
        vt_ref[h, ATTN_HEAD_DIM:, :] = jnp.ones((_VT_ROWS - ATTN_HEAD_DIM, tm), vt_ref.dtype)
    qi_ref[...] = seg(_C_QI, IDX_HEADS * IDX_DIM).astype(qi_ref.dtype)
    kw_ref[...] = seg(_C_KW, LANES)
    gb_ref[...] = jax.nn.sigmoid(seg(_C_GB, D_MODEL))

    u_sc[...] = jax.nn.gelu(seg(_C_U, D_MODEL))
    vn = _layer_norm(jax.nn.gelu(seg(_C_V, D_MODEL)), lng_ref[...], lnb_ref[...])
    vn_sc[...] = vn.astype(vn_sc.dtype)
    row = lax.broadcasted_iota(jnp.int32, (CHUNK, CHUNK), 0)
    col = lax.broadcasted_iota(jnp.int32, (CHUNK, CHUNK), 1)
    causal = col <= row
    for g in range(SGU_GROUPS):
        wm = jnp.where(causal, sw_ref[g], 0.0).astype(_MXU_DTYPE)
        bias = sbt_ref[:, g:g + 1]
        cs = slice(g * SGU_GROUP_DIM, (g + 1) * SGU_GROUP_DIM)
        for c in range(tm // CHUNK):
            rs = slice(c * CHUNK, (c + 1) * CHUNK)
            s = jnp.dot(wm, vn_sc[rs, cs], preferred_element_type=jnp.float32) + bias
            a_sc[rs, cs] = (u_sc[rs, cs] * s).astype(a_sc.dtype)

    branch_a = jnp.dot(a_sc[...], wa_ref[...], preferred_element_type=jnp.float32)
    ma_ref[...] = jax.nn.sigmoid(seg(_C_GA, D_MODEL)) * branch_a


def _proj_call(x2, w_packed, w_vt, ln_g, ln_b, sgu_w, sgu_bt, w_a, bsz, seq):
    n = x2.shape[0]
    tm = _TM_PROJ
    row_spec = lambda width: pl.BlockSpec((tm, width), lambda i: (i, 0))
    out_shapes = (
        jax.ShapeDtypeStruct((n, ATTN_WIDTH), _MXU_DTYPE),
        jax.ShapeDtypeStruct((n, ATTN_WIDTH), _MXU_DTYPE),
        jax.ShapeDtypeStruct((n // tm, ATTN_HEADS, _VT_ROWS, tm), _MXU_DTYPE),
        jax.ShapeDtypeStruct((n, IDX_HEADS * IDX_DIM), _MXU_DTYPE),
        jax.ShapeDtypeStruct((n, LANES), jnp.float32),
        jax.ShapeDtypeStruct((n, D_MODEL), jnp.float32),
        jax.ShapeDtypeStruct((n, D_MODEL), jnp.float32),
    )
    vt_spec = pl.BlockSpec((None, ATTN_HEADS, _VT_ROWS, tm), lambda i: (i, 0, 0, 0))
    return pl.pallas_call(
        _proj_kernel,
        grid=(n // tm,),
        in_specs=[
            row_spec(D_MODEL),
            _resident(w_packed.shape),
            _resident(w_vt.shape),
            _resident(ln_g.shape),
            _resident(ln_b.shape),
            _resident(sgu_w.shape),
            _resident(sgu_bt.shape),
            _resident(w_a.shape),
        ],
        out_specs=[row_spec(ATTN_WIDTH), row_spec(ATTN_WIDTH), vt_spec,
                   row_spec(IDX_HEADS * IDX_DIM), row_spec(LANES), row_spec(D_MODEL), row_spec(D_MODEL)],
        out_shape=out_shapes,
        scratch_shapes=[
            pltpu.VMEM((tm, D_MODEL), jnp.float32),
            pltpu.VMEM((tm, D_MODEL), _MXU_DTYPE),
            pltpu.VMEM((tm, D_MODEL), _MXU_DTYPE),
        ],
        compiler_params=pltpu.CompilerParams(
            dimension_semantics=("parallel",), vmem_limit_bytes=VMEM_LIMIT_BYTES),
        name="proj",
    )(x2, w_packed, w_vt, ln_g, ln_b, sgu_w, sgu_bt, w_a)


def _key_to_f32(key):
    bits = jnp.where(key < 0, key & jnp.int32(0x7FFFFFFF), ~key)
    return lax.bitcast_convert_type(bits, jnp.float32)


def _key16_to_f32(key16):
    pattern = jnp.where(key16 >= 32768, key16 - 32768, 65535 - key16)
    return lax.bitcast_convert_type(lax.shift_left(pattern, 16), jnp.float32)


_I32_MIN = -2 ** 31
_COARSE_DTYPE = jnp.bfloat16
_BF16_KEY_STEP = 1 << 16
_FINE_STEPS = 17
_FINE_ALWAYS = 5
_FINE_GROUP = 4


def _index_kernel(qi_ref, kwq_ref, kwk_ref, mask_ref, sc_ref, hb_ref, *, top_k):
    tq = qi_ref.shape[0]
    seq = kwk_ref.shape[0]
    sk = min(_SK_IDX, seq)
    rows = _ROWS_IDX
    i = pl.program_id(1)
    col0 = i * tq
    nkb = (col0 + tq + sk - 1) // sk
    t_row = col0 + lax.broadcasted_iota(jnp.int32, (1, tq), 1)
    w8 = jnp.transpose(kwq_ref[...])[IDX_DIM:IDX_DIM + IDX_HEADS, :] * (IDX_HEADS ** -0.5 * IDX_DIM ** -0.5)
    lane = lax.broadcasted_iota(jnp.int32, (sk, LANES), 1)

    def score_body(kb, carry):
        ks = pl.multiple_of(kb * sk, sk)
        kblk = jnp.where(lane < IDX_DIM, kwk_ref[pl.ds(ks, sk), :], 0.0)
        k_lo = kblk.astype(_MXU_DTYPE)
        k_hi = pltpu.roll(kblk, IDX_DIM, axis=1).astype(_MXU_DTYPE)
        blk = sc_ref.at[pl.ds(ks, sk), :]
        for hp in range(IDX_HEADS // 2):
            qpair = qi_ref[:, hp * LANES:(hp + 1) * LANES]
            for half, kk in enumerate((k_lo, k_hi)):
                h = 2 * hp + half
                logits = lax.dot_general(kk, qpair, _NT_DIMS, preferred_element_type=jnp.float32)
                term = w8[h:h + 1, :] * jnp.maximum(logits, 0.0)
                if h == 0:
                    blk[...] = term
                elif h < IDX_HEADS - 1:
                    blk[...] = blk[...] + term
                else:
                    s_col = ks + lax.broadcasted_iota(jnp.int32, (sk, tq), 0)
                    score = jnp.where(s_col <= t_row, blk[...] + term, -jnp.inf)
                    blk[...] = score
                    hb_ref[pl.ds(ks, sk), :] = score.astype(_COARSE_DTYPE)
        return carry

    lax.fori_loop(0, nkb, score_body, 0)

    def for_offsets(fn, init):
        def body(kb, carry):
            ks = pl.multiple_of(kb * sk, sk)
            for j in range(sk // rows):
                carry = fn(carry, pl.multiple_of(ks + j * rows, rows))
            return carry
        return lax.fori_loop(0, nkb, body, init)

    def for_tiles(fn, init):
        return for_offsets(lambda carry, off: fn(carry, sc_ref[pl.ds(off, rows), :], off), init)

    def count(indicator):
        acc = for_tiles(lambda acc, tile, off: acc + indicator(tile, off), jnp.zeros((rows, tq), jnp.float32))
        return jnp.sum(acc, axis=0, keepdims=True)

    keep_all = t_row < top_k

    def coarse_step(it, key16):
        trial = key16 | lax.shift_left(jnp.int32(1), 15 - it)
        cand = jnp.broadcast_to(_key16_to_f32(trial), (rows, tq)).astype(_COARSE_DTYPE)
        def add_tile(acc, off):
            ge = hb_ref[pl.ds(off, rows), :] >= cand
            return acc + jnp.where(ge, _COARSE_DTYPE(1), _COARSE_DTYPE(0))
        acc = for_offsets(add_tile, jnp.zeros((rows, tq), _COARSE_DTYPE))
        cnt = jnp.sum(acc.astype(jnp.float32), axis=0, keepdims=True)
        return jnp.where(cnt >= top_k, trial, key16)

    key16 = lax.fori_loop(0, 16, coarse_step, jnp.zeros((1, tq), jnp.int32))
    t16_bits = lax.bitcast_convert_type(_key16_to_f32(key16), jnp.int32)
    t16_key = jnp.where(t16_bits < 0, ~t16_bits, t16_bits | jnp.int32(_I32_MIN))
    key_base = t16_key - _BF16_KEY_STEP

    def fine_step(it, carry):
        offset, cnt_key = carry
        trial = offset | lax.shift_left(jnp.int32(1), _FINE_STEPS - 1 - it)
        cand = jnp.broadcast_to(_key_to_f32(key_base + trial), (rows, tq))
        cnt = count(lambda tile, off: jnp.where(tile >= cand, 1.0, 0.0))
        ok = cnt >= top_k
        return jnp.where(ok, trial, offset), jnp.where(ok, cnt, cnt_key)

    state = (jnp.zeros((1, tq), jnp.int32), jnp.full((1, tq), float(seq), jnp.float32))
    state = lax.fori_loop(0, _FINE_ALWAYS, fine_step, state)
    for start in range(_FINE_ALWAYS, _FINE_STEPS, _FINE_GROUP):
        stop = min(start + _FINE_GROUP, _FINE_STEPS)
        unresolved = jnp.where(keep_all, 0.0, jnp.where(state[1] > top_k, 1.0, 0.0))
        state = lax.cond(
            jnp.max(unresolved) > 0.0,
            lambda st, start=start, stop=stop: lax.fori_loop(start, stop, fine_step, st),
            lambda st: st,
            state)
    offset, n_ge = state
    tau = jnp.where(keep_all, -jnp.inf, _key_to_f32(key_base + offset))
    tau_b = jnp.broadcast_to(tau, (rows, tq))

    excess = jnp.where(keep_all, 0.0, jnp.where(n_ge > top_k, 1.0, 0.0))
    sub_row = lax.broadcasted_iota(jnp.int32, (rows, tq), 0)
    t_b = jnp.broadcast_to(t_row, (rows, tq))

    def write_with_ties():
        def mark_tied(acc, tile, off):
            eq = jnp.where(tile == tau_b, 1.0, 0.0)
            hb_ref[pl.ds(off, rows), :] = eq.astype(_COARSE_DTYPE)
            return acc + eq
        n_eq = jnp.sum(for_tiles(mark_tied, jnp.zeros((rows, tq), jnp.float32)), axis=0, keepdims=True)
        need = top_k - (n_ge - n_eq)
        r = lax.broadcasted_iota(jnp.int32, (sk, sk), 0)
        c = lax.broadcasted_iota(jnp.int32, (sk, sk), 1)
        tri = jnp.where(c <= r, 1.0, 0.0).astype(_COARSE_DTYPE)
        tau_blk = jnp.broadcast_to(tau, (sk, tq))
        def block(kb, taken):
            ks = pl.multiple_of(kb * sk, sk)
            rank = taken + jnp.dot(tri, hb_ref[pl.ds(ks, sk), :], preferred_element_type=jnp.float32)
            tile = sc_ref[pl.ds(ks, sk), :]
            tied = jnp.where(tile == tau_blk, jnp.where(rank <= need, 1, 0), 0)
            s_col = ks + lax.broadcasted_iota(jnp.int32, (sk, tq), 0)
            sel = jnp.where(s_col <= t_row, jnp.where(tile > tau_blk, 1, tied), 0)
            mask_ref[pl.ds(ks, sk), :] = sel.astype(mask_ref.dtype)
            return rank[sk - 1:sk, :]
        lax.fori_loop(0, nkb, block, jnp.zeros((1, tq), jnp.float32))

    def write_without_ties():
        def write_mask(carry, tile, off):
            sel = jnp.where(off + sub_row <= t_b, jnp.where(tile >= tau_b, 1, 0), 0)
            mask_ref[pl.ds(off, rows), :] = sel.astype(mask_ref.dtype)
            return carry
        for_tiles(write_mask, 0)

    lax.cond(jnp.max(excess) > 0.0, write_with_ties, write_without_ties)

    def zero_body(kb, carry):
        ks = pl.multiple_of(kb * sk, sk)
        mask_ref[pl.ds(ks, sk), :] = jnp.zeros((sk, tq), mask_ref.dtype)
        return carry

    lax.fori_loop(nkb, seq // sk, zero_body, 0)


def _index_call(qidx, kw, bsz, seq, top_k):
    tq = min(_TQ_IDX, seq)
    nq = seq // tq
    return pl.pallas_call(
        functools.partial(_index_kernel, top_k=top_k),
        grid=(bsz, nq),
        in_specs=[
            pl.BlockSpec((tq, IDX_HEADS * IDX_DIM), lambda b, i: (b * nq + i, 0)),
            pl.BlockSpec((tq, LANES), lambda b, i: (b * nq + i, 0)),
            pl.BlockSpec((seq, LANES), lambda b, i: (b, 0)),
        ],
        out_specs=pl.BlockSpec((None, seq, tq), lambda b, i: (b * nq + i, 0, 0)),
        out_shape=jax.ShapeDtypeStruct((bsz * nq, seq, tq), jnp.int8),
        scratch_shapes=[
            pltpu.VMEM((seq, tq), jnp.float32),
            pltpu.VMEM((seq, tq), _COARSE_DTYPE),
        ],
        compiler_params=pltpu.CompilerParams(
            dimension_semantics=("parallel", "parallel"), vmem_limit_bytes=VMEM_LIMIT_BYTES),
        name="index",
    )(qidx, kw, kw)


def _attn_kernel(q_ref, k_ref, vt_ref, mask_ref, o_ref, m_sc, acc_sc, s_sc, bias_sc):
    tq = q_ref.shape[0]
    seq = k_ref.shape[0]
    sk = min(_SK_ATT, seq)
    i = pl.program_id(1)
    nkb = (i * tq + tq + sk - 1) // sk
    last_kb = seq // sk - 1

    m_sc[...] = jnp.full(m_sc.shape, _NEG, jnp.float32)
    acc_sc[...] = jnp.zeros(acc_sc.shape, jnp.float32)

    def stage_bias(kb):
        ks = pl.multiple_of(kb * sk, sk)
        bias_sc[kb % 2] = (1.0 - mask_ref[pl.ds(ks, sk), :].astype(jnp.float32)) * _NEG

    def logits(kb, h):
        ks = pl.multiple_of(kb * sk, sk)
        hs = slice(h * ATTN_HEAD_DIM, (h + 1) * ATTN_HEAD_DIM)
        half = sk // 2
        mx = None
        for part in range(2):
            rs = slice(part * half, (part + 1) * half)
            s = lax.dot_general(k_ref[pl.ds(ks + part * half, half), hs], q_ref[:, hs], _NT_DIMS,
                                preferred_element_type=jnp.float32) + bias_sc[kb % 2, rs, :]
            s_sc[h % 2, rs, :] = s
            part_max = jnp.max(s, axis=0, keepdims=True)
            mx = part_max if mx is None else jnp.maximum(mx, part_max)
        return mx

    def body(kb, mx_next):
        ks = pl.multiple_of(kb * sk, sk)
        kb_next = jnp.minimum(kb + 1, last_kb)
        stage_bias(kb_next)
        for h in range(ATTN_HEADS):
            mx = mx_next
            if h + 1 < ATTN_HEADS:
                mx_next = logits(kb, h + 1)
            else:
                mx_next = logits(kb_next, 0)
            m_old = m_sc[h]
            m_new = jnp.maximum(m_old, mx)
            alpha = jnp.exp2(m_old - m_new)
            half = sk // 2
            pv = None
            for part in range(2):
                rs = slice(part * half, (part + 1) * half)
                p = jnp.exp2(s_sc[h % 2, rs, :] - m_new).astype(_MXU_DTYPE)
                part_pv = jnp.dot(vt_ref[kb, h, :, rs], p,
                                  preferred_element_type=jnp.float32)
                pv = part_pv if pv is None else pv + part_pv
            acc_sc[h] = alpha * acc_sc[h] + pv
            m_sc[h] = m_new
        return mx_next

    stage_bias(0)
    lax.fori_loop(0, nkb, body, logits(0, 0))
    for h in range(ATTN_HEADS):
        hs = slice(h * ATTN_HEAD_DIM, (h + 1) * ATTN_HEAD_DIM)
        acc = acc_sc[h]
        out_t = acc[:ATTN_HEAD_DIM, :] / acc[ATTN_HEAD_DIM:ATTN_HEAD_DIM + 1, :]
        o_ref[:, hs] = jnp.transpose(out_t).astype(o_ref.dtype)


def _attn_call(q, k, vt, mask, bsz, seq):
    tq = min(_TQ_ATT, seq)
    nq = seq // tq
    sk = min(_SK_ATT, seq)
    assert vt.shape[-1] == sk and mask.shape[-1] == tq
    return pl.pallas_call(
        _attn_kernel,
        grid=(bsz, nq),
        in_specs=[
            pl.BlockSpec((tq, ATTN_WIDTH), lambda b, i: (b * nq + i, 0)),
            pl.BlockSpec((seq, ATTN_WIDTH), lambda b, i: (b, 0)),
            pl.BlockSpec((seq // sk, ATTN_HEADS, _VT_ROWS, sk), lambda b, i: (b, 0, 0, 0)),
            pl.BlockSpec((None, seq, tq), lambda b, i: (b * nq + i, 0, 0)),
        ],
        out_specs=pl.BlockSpec((tq, ATTN_WIDTH), lambda b, i: (b * nq + i, 0)),
        out_shape=jax.ShapeDtypeStruct((bsz * seq, ATTN_WIDTH), _MXU_DTYPE),
        scratch_shapes=[
            pltpu.VMEM((ATTN_HEADS, 1, tq), jnp.float32),
            pltpu.VMEM((ATTN_HEADS, _VT_ROWS, tq), jnp.float32),
            pltpu.VMEM((2, min(_SK_ATT, seq), tq), jnp.float32),
            pltpu.VMEM((2, min(_SK_ATT, seq), tq), jnp.float32),
        ],
        compiler_params=pltpu.CompilerParams(
            dimension_semantics=("parallel", "parallel"), vmem_limit_bytes=VMEM_LIMIT_BYTES),
        name="attn",
    )(q, k, vt, mask)


def _out_kernel(x_ref, ma_ref, gb_ref, b_ref, wb_ref, wo_ref, g1_ref, b1_ref,
                wup_ref, wdn_ref, g2_ref, b2_ref, o_ref, *, alpha):
    branch_b = jnp.dot(b_ref[...], wb_ref[...], preferred_element_type=jnp.float32)
    merged = ma_ref[...] + gb_ref[...] * branch_b
    y = alpha * x_ref[...] + jnp.dot(merged.astype(_MXU_DTYPE), wo_ref[...],
                                     preferred_element_type=jnp.float32)
    x1 = _layer_norm(y, g1_ref[...], b1_ref[...])
    x1b = x1.astype(_MXU_DTYPE)
    acc = jnp.zeros(x1.shape, jnp.float32)
    for c in range(FFN_DIM // _FFN_CHUNK):
        cs = slice(c * _FFN_CHUNK, (c + 1) * _FFN_CHUNK)
        h = jnp.square(jnp.maximum(jnp.dot(x1b, wup_ref[:, cs], preferred_element_type=jnp.float32), 0.0))
        acc = acc + jnp.dot(h.astype(_MXU_DTYPE), wdn_ref[cs, :], preferred_element_type=jnp.float32)
    o_ref[...] = _layer_norm(alpha * x1 + acc, g2_ref[...], b2_ref[...])


def _out_call(x2, ma, gb, battn, w_b, w_o, g1, b1, w_up, w_dn, g2, b2, alpha):
    n = x2.shape[0]
    tm = _TM_OUT
    row_spec = pl.BlockSpec((tm, D_MODEL), lambda i: (i, 0))
    return pl.pallas_call(
        functools.partial(_out_kernel, alpha=alpha),
        grid=(n // tm,),
        in_specs=[row_spec, row_spec, row_spec, row_spec,
                  _resident(w_b.shape), _resident(w_o.shape), _resident(g1.shape), _resident(b1.shape),
                  _resident(w_up.shape), _resident(w_dn.shape), _resident(g2.shape), _resident(b2.shape)],
        out_specs=row_spec,
        out_shape=jax.ShapeDtypeStruct((n, D_MODEL), jnp.float32),
        compiler_params=pltpu.CompilerParams(
            dimension_semantics=("parallel",), vmem_limit_bytes=VMEM_LIMIT_BYTES),
        name="out",
    )(x2, ma, gb, battn, w_b, w_o, g1, b1, w_up, w_dn, g2, b2)


def _pack_w_in(w_in):
    offs = [0]
    for width in IN_SPLITS:
        offs.append(offs[-1] + width)
    u, v, q, k, va, qi, ki, wi, ga, gb = (w_in[:, offs[j]:offs[j + 1]] for j in range(len(IN_SPLITS)))
    pad = jnp.zeros((w_in.shape[0], LANES - IDX_DIM - IDX_HEADS), w_in.dtype)
    packed = jnp.concatenate([u, v, q, k, ga, gb, qi, ki, wi, pad], axis=1).astype(_MXU_DTYPE)
    return packed, va.T.astype(_MXU_DTYPE)


def _layer(x, w_in, sgu_ln_g, sgu_ln_b, sgu_w, sgu_b, w_branch_a, w_branch_b, w_out,
           ln1_g, ln1_b, w_ffn_up, w_ffn_down, ln2_g, ln2_b, alpha):
    bsz, seq, _ = x.shape
    assert seq % _TM_PROJ == 0 and seq % _TM_OUT == 0 and _TM_PROJ % CHUNK == 0
    assert seq % min(_TQ_ATT, seq) == 0 and seq % min(_SK_ATT, seq) == 0
    assert seq % min(_TQ_IDX, seq) == 0 and seq % min(_SK_IDX, seq) == 0
    top_k = min(IDX_TOPK_MAX, seq // 4)
    x2 = x.reshape(bsz * seq, D_MODEL)
    vec = lambda a: a.reshape(1, -1)
    cast = lambda a: a.astype(_MXU_DTYPE)

    w_packed, w_vt = _pack_w_in(w_in)
    q, k, vt, qidx, kw, ma, gb = _proj_call(
        x2, w_packed, w_vt, vec(sgu_ln_g), vec(sgu_ln_b), sgu_w, sgu_b.T, cast(w_branch_a), bsz, seq)
    mask = _index_call(qidx, kw, bsz, seq, top_k)
    battn = _attn_call(q, k, vt, mask, bsz, seq)
    out = _out_call(x2, ma, gb, battn, cast(w_branch_b), cast(w_out), vec(ln1_g), vec(ln1_b),
                    cast(w_ffn_up), cast(w_ffn_down), vec(ln2_g), vec(ln2_b), alpha)
    return out.reshape(bsz, seq, D_MODEL)


def kernel(x, w_in, sgu_ln_g, sgu_ln_b, sgu_w, sgu_b, w_branch_a, w_branch_b, w_out,
           ln1_g, ln1_b, w_ffn_up, w_ffn_down, ln2_g, ln2_b):
    depth = w_in.shape[0]
    alpha = (2.0 * depth) ** 0.25
    for i in range(depth):
        x = _layer(x, w_in[i], sgu_ln_g[i], sgu_ln_b[i], sgu_w[i], sgu_b[i], w_branch_a[i],
                   w_branch_b[i], w_out[i], ln1_g[i], ln1_b[i], w_ffn_up[i], w_ffn_down[i],
                   ln2_g[i], ln2_b[i], alpha)
    return x
```

```python
import functools

import jax
import jax.numpy as jnp
from jax import lax
from jax.experimental import pallas as pl
from jax.experimental.pallas import tpu as pltpu

D_MODEL = 1024
CHUNK = 128
SGU_GROUPS = 8
SGU_GROUP_DIM = D_MODEL // SGU_GROUPS
ATTN_HEADS = 8
ATTN_HEAD_DIM = 128
ATTN_WIDTH = ATTN_HEADS * ATTN_HEAD_DIM
IDX_HEADS = 8
IDX_DIM = 64
IDX_TOPK_MAX = 256
FFN_DIM = 4 * D_MODEL
LN_EPS = 1e-5
IN_SPLITS = (D_MODEL, D_MODEL, ATTN_WIDTH, ATTN_WIDTH, ATTN_WIDTH,
             IDX_HEADS * IDX_DIM, IDX_DIM, IDX_HEADS, D_MODEL, D_MODEL)

LANES = 128
VMEM_LIMIT_BYTES = 56 * 2**20

_MXU_DTYPE = jnp.bfloat16
_NEG = -1e30
_LOG2_E = 1.4426950408889634

_C_U, _C_V, _C_K, _C_GA, _C_GB = (i * D_MODEL for i in range(5))
_C_KW = 5 * D_MODEL
_R_V, _R_Q, _R_QI, _R_END = 0, ATTN_WIDTH, 2 * ATTN_WIDTH, 2 * ATTN_WIDTH + IDX_HEADS * IDX_DIM

_TM_PROJ = 512
_TQ_IDX = 512
_SK_IDX = 512
_ROWS_IDX = 32
_TQ_ATT = 512
_SK_ATT = 512
_TM_OUT = 512
_FFN_CHUNK = 1024

_NT_DIMS = (((1,), (1,)), ((), ()))
_VT_ROWS = ATTN_HEAD_DIM + 16


def _layer_norm(x, g, b):
    mu = jnp.mean(x, axis=-1, keepdims=True)
    xc = x - mu
    var = jnp.mean(xc * xc, axis=-1, keepdims=True)
    return xc * lax.rsqrt(var + LN_EPS) * g + b


def _resident(shape):
    nd = len(shape)
    return pl.BlockSpec(shape, lambda *_: (0,) * nd, pipeline_mode=pl.Buffered(1))


def _proj_kernel(x_ref, w_ref, wt_ref, lng_ref, lnb_ref, sw_ref, sbt_ref, wa_ref,
                 qt_ref, k_ref, vt_ref, qit_ref, kw_ref, ma_ref, gb_ref,
                 u_sc, vn_sc, a_sc):
    tm = x_ref.shape[0]
    xb = x_ref[...].astype(_MXU_DTYPE)

    def seg(lo, width):
        return jnp.dot(xb, w_ref[:, lo:lo + width], preferred_element_type=jnp.float32)

    def seg_t(lo, hi):
        return lax.dot_general(wt_ref[lo:hi, :], xb, _NT_DIMS, preferred_element_type=jnp.float32)

    k_ref[...] = seg(_C_K, ATTN_WIDTH).astype(k_ref.dtype)
    qt_ref[...] = (seg_t(_R_Q, _R_QI) * (ATTN_HEAD_DIM ** -0.5 * _LOG2_E)).astype(qt_ref.dtype)
    qit_ref[...] = seg_t(_R_QI, _R_END).astype(qit_ref.dtype)
    vt = seg_t(_R_V, _R_Q)
    for h in range(ATTN_HEADS):
        vt_ref[h, :ATTN_HEAD_DIM, :] = vt[h * ATTN_HEAD_DIM:(h + 1) * ATTN_HEAD_DIM, :].astype(vt_ref.dtype)
        vt_ref[h, ATTN_HEAD_DIM:, :] = jnp.ones((_VT_ROWS - ATTN_HEAD_DIM, tm), vt_ref.dtype)
    kw_ref[...] = seg(_C_KW, LANES)
    gb_ref[...] = jax.nn.sigmoid(seg(_C_GB, D_MODEL))

    u_sc[...] = jax.nn.gelu(seg(_C_U, D_MODEL))
    vn = _layer_norm(jax.nn.gelu(seg(_C_V, D_MODEL)), lng_ref[...], lnb_ref[...])
    vn_sc[...] = vn.astype(vn_sc.dtype)
    row = lax.broadcasted_iota(jnp.int32, (CHUNK, CHUNK), 0)
    col = lax.broadcasted_iota(jnp.int32, (CHUNK, CHUNK), 1)
    causal = col <= row
    for g in range(SGU_GROUPS):
        wm = jnp.where(causal, sw_ref[g], 0.0).astype(_MXU_DTYPE)
        bias = sbt_ref[:, g:g + 1]
        cs = slice(g * SGU_GROUP_DIM, (g + 1) * SGU_GROUP_DIM)
        for c in range(tm // CHUNK):
            rs = slice(c * CHUNK, (c + 1) * CHUNK)
            s = jnp.dot(wm, vn_sc[rs, cs], preferred_element_type=jnp.float32) + bias
            a_sc[rs, cs] = (u_sc[rs, cs] * s).astype(a_sc.dtype)

    branch_a = jnp.dot(a_sc[...], wa_ref[...], preferred_element_type=jnp.float32)
    ma_ref[...] = jax.nn.sigmoid(seg(_C_GA, D_MODEL)) * branch_a


def _proj_call(x2, w_packed, w_t, ln_g, ln_b, sgu_w, sgu_bt, w_a):
    n = x2.shape[0]
    tm = _TM_PROJ
    row_spec = lambda width: pl.BlockSpec((tm, width), lambda i: (i, 0))
    out_shapes = (
        jax.ShapeDtypeStruct((n // tm, ATTN_WIDTH, tm), _MXU_DTYPE),
        jax.ShapeDtypeStruct((n, ATTN_WIDTH), _MXU_DTYPE),
        jax.ShapeDtypeStruct((n // tm, ATTN_HEADS, _VT_ROWS, tm), _MXU_DTYPE),
        jax.ShapeDtypeStruct((n // tm, IDX_HEADS * IDX_DIM, tm), _MXU_DTYPE),
        jax.ShapeDtypeStruct((n, LANES), jnp.float32),
        jax.ShapeDtypeStruct((n, D_MODEL), jnp.float32),
        jax.ShapeDtypeStruct((n, D_MODEL), jnp.float32),
    )
    vt_spec = pl.BlockSpec((None, ATTN_HEADS, _VT_ROWS, tm), lambda i: (i, 0, 0, 0))
    slab_spec = lambda rows: pl.BlockSpec((None, rows, tm), lambda i: (i, 0, 0))
    return pl.pallas_call(
        _proj_kernel,
        grid=(n // tm,),
        in_specs=[
            row_spec(D_MODEL),
            _resident(w_packed.shape),
            _resident(w_t.shape),
            _resident(ln_g.shape),
            _resident(ln_b.shape),
            _resident(sgu_w.shape),
            _resident(sgu_bt.shape),
            _resident(w_a.shape),
        ],
        out_specs=[slab_spec(ATTN_WIDTH), row_spec(ATTN_WIDTH), vt_spec,
                   slab_spec(IDX_HEADS * IDX_DIM), row_spec(LANES), row_spec(D_MODEL), row_spec(D_MODEL)],
        out_shape=out_shapes,
        scratch_shapes=[
            pltpu.VMEM((tm, D_MODEL), jnp.float32),
            pltpu.VMEM((tm, D_MODEL), _MXU_DTYPE),
            pltpu.VMEM((tm, D_MODEL), _MXU_DTYPE),
        ],
        compiler_params=pltpu.CompilerParams(
            dimension_semantics=("parallel",), vmem_limit_bytes=VMEM_LIMIT_BYTES),
        name="proj",
    )(x2, w_packed, w_t, ln_g, ln_b, sgu_w, sgu_bt, w_a)


def _key_to_f32(key):
    bits = jnp.where(key < 0, key & jnp.int32(0x7FFFFFFF), ~key)
    return lax.bitcast_convert_type(bits, jnp.float32)


def _key16_to_f32(key16):
    pattern = jnp.where(key16 >= 32768, key16 - 32768, 65535 - key16)
    return lax.bitcast_convert_type(lax.shift_left(pattern, 16), jnp.float32)


_I32_MIN = -2 ** 31
_COARSE_DTYPE = jnp.bfloat16
_BF16_KEY_STEP = 1 << 16
_FINE_STEPS = 17
_FINE_ALWAYS = 5
_FINE_GROUP = 4


def _index_kernel(qit_ref, kwq_ref, kwk_ref, mask_ref, sc_ref, hb_ref, *, top_k):
    tq = qit_ref.shape[1]
    seq = kwk_ref.shape[0]
    sk = min(_SK_IDX, seq)
    rows = _ROWS_IDX
    i = pl.program_id(1)
    col0 = i * tq
    nkb = (col0 + tq + sk - 1) // sk
    t_row = col0 + lax.broadcasted_iota(jnp.int32, (1, tq), 1)
    w8 = jnp.transpose(kwq_ref[...])[IDX_DIM:IDX_DIM + IDX_HEADS, :] * (IDX_HEADS ** -0.5 * IDX_DIM ** -0.5)
    lane = lax.broadcasted_iota(jnp.int32, (sk, LANES), 1)

    def score_body(kb, carry):
        ks = pl.multiple_of(kb * sk, sk)
        kblk = jnp.where(lane < IDX_DIM, kwk_ref[pl.ds(ks, sk), :], 0.0)
        k_lo = kblk.astype(_MXU_DTYPE)
        k_hi = pltpu.roll(kblk, IDX_DIM, axis=1).astype(_MXU_DTYPE)
        blk = sc_ref.at[pl.ds(ks, sk), :]
        for hp in range(IDX_HEADS // 2):
            qpair = qit_ref[hp * LANES:(hp + 1) * LANES, :]
            for half, kk in enumerate((k_lo, k_hi)):
                h = 2 * hp + half
                logits = jnp.dot(kk, qpair, preferred_element_type=jnp.float32)
                term = w8[h:h + 1, :] * jnp.maximum(logits, 0.0)
                if h == 0:
                    blk[...] = term
                elif h < IDX_HEADS - 1:
                    blk[...] = blk[...] + term
                else:
                    s_col = ks + lax.broadcasted_iota(jnp.int32, (sk, tq), 0)
                    score = jnp.where(s_col <= t_row, blk[...] + term, -jnp.inf)
                    blk[...] = score
                    hb_ref[pl.ds(ks, sk), :] = score.astype(_COARSE_DTYPE)
        return carry

    lax.fori_loop(0, nkb, score_body, 0)

    def for_offsets(fn, init):
        def body(kb, carry):
            ks = pl.multiple_of(kb * sk, sk)
            for j in range(sk // rows):
                carry = fn(carry, pl.multiple_of(ks + j * rows, rows))
            return carry
        return lax.fori_loop(0, nkb, body, init)

    def for_tiles(fn, init):
        return for_offsets(lambda carry, off: fn(carry, sc_ref[pl.ds(off, rows), :], off), init)

    def count(indicator):
        acc = for_tiles(lambda acc, tile, off: acc + indicator(tile, off), jnp.zeros((rows, tq), jnp.float32))
        return jnp.sum(acc, axis=0, keepdims=True)

    keep_all = t_row < top_k

    def coarse_step(it, key16):
        trial = key16 | lax.shift_left(jnp.int32(1), 15 - it)
        cand = jnp.broadcast_to(_key16_to_f32(trial), (rows, tq)).astype(_COARSE_DTYPE)
        def add_tile(acc, off):
            ge = hb_ref[pl.ds(off, rows), :] >= cand
            return acc + jnp.where(ge, _COARSE_DTYPE(1), _COARSE_DTYPE(0))
        acc = for_offsets(add_tile, jnp.zeros((rows, tq), _COARSE_DTYPE))
        cnt = jnp.sum(acc.astype(jnp.float32), axis=0, keepdims=True)
        return jnp.where(cnt >= top_k, trial, key16)

    key16 = lax.fori_loop(0, 16, coarse_step, jnp.zeros((1, tq), jnp.int32))
    t16_bits = lax.bitcast_convert_type(_key16_to_f32(key16), jnp.int32)
    t16_key = jnp.where(t16_bits < 0, ~t16_bits, t16_bits | jnp.int32(_I32_MIN))
    key_base = t16_key - _BF16_KEY_STEP

    def fine_step(it, carry):
        offset, cnt_key = carry
        trial = offset | lax.shift_left(jnp.int32(1), _FINE_STEPS - 1 - it)
        cand = jnp.broadcast_to(_key_to_f32(key_base + trial), (rows, tq))
        cnt = count(lambda tile, off: jnp.where(tile >= cand, 1.0, 0.0))
        ok = cnt >= top_k
        return jnp.where(ok, trial, offset), jnp.where(ok, cnt, cnt_key)

    state = (jnp.zeros((1, tq), jnp.int32), jnp.full((1, tq), float(seq), jnp.float32))
    state = lax.fori_loop(0, _FINE_ALWAYS, fine_step, state)
    for start in range(_FINE_ALWAYS, _FINE_STEPS, _FINE_GROUP):
        stop = min(start + _FINE_GROUP, _FINE_STEPS)
        unresolved = jnp.where(keep_all, 0.0, jnp.where(state[1] > top_k, 1.0, 0.0))
        state = lax.cond(
            jnp.max(unresolved) > 0.0,
            lambda st, start=start, stop=stop: lax.fori_loop(start, stop, fine_step, st),
            lambda st: st,
            state)
    offset, n_ge = state
    tau = jnp.where(keep_all, -jnp.inf, _key_to_f32(key_base + offset))
    tau_b = jnp.broadcast_to(tau, (rows, tq))

    excess = jnp.where(keep_all, 0.0, jnp.where(n_ge > top_k, 1.0, 0.0))
    sub_row = lax.broadcasted_iota(jnp.int32, (rows, tq), 0)
    t_b = jnp.broadcast_to(t_row, (rows, tq))

    def write_with_ties():
        def mark_tied(acc, tile, off):
            eq = jnp.where(tile == tau_b, 1.0, 0.0)
            hb_ref[pl.ds(off, rows), :] = eq.astype(_COARSE_DTYPE)
            return acc + eq
        n_eq = jnp.sum(for_tiles(mark_tied, jnp.zeros((rows, tq), jnp.float32)), axis=0, keepdims=True)
        need = top_k - (n_ge - n_eq)
        r = lax.broadcasted_iota(jnp.int32, (sk, sk), 0)
        c = lax.broadcasted_iota(jnp.int32, (sk, sk), 1)
        tri = jnp.where(c <= r, 1.0, 0.0).astype(_COARSE_DTYPE)
        tau_blk = jnp.broadcast_to(tau, (sk, tq))
        def block(kb, taken):
            ks = pl.multiple_of(kb * sk, sk)
            rank = taken + jnp.dot(tri, hb_ref[pl.ds(ks, sk), :], preferred_element_type=jnp.float32)
            tile = sc_ref[pl.ds(ks, sk), :]
            tied = jnp.where(tile == tau_blk, jnp.where(rank <= need, 1, 0), 0)
            s_col = ks + lax.broadcasted_iota(jnp.int32, (sk, tq), 0)
            sel = jnp.where(s_col <= t_row, jnp.where(tile > tau_blk, 1, tied), 0)
            mask_ref[pl.ds(ks, sk), :] = sel.astype(mask_ref.dtype)
            return rank[sk - 1:sk, :]
        lax.fori_loop(0, nkb, block, jnp.zeros((1, tq), jnp.float32))

    def write_without_ties():
        def write_mask(carry, tile, off):
            sel = jnp.where(off + sub_row <= t_b, jnp.where(tile >= tau_b, 1, 0), 0)
            mask_ref[pl.ds(off, rows), :] = sel.astype(mask_ref.dtype)
            return carry
        for_tiles(write_mask, 0)

    lax.cond(jnp.max(excess) > 0.0, write_with_ties, write_without_ties)

    def zero_body(kb, carry):
        ks = pl.multiple_of(kb * sk, sk)
        mask_ref[pl.ds(ks, sk), :] = jnp.zeros((sk, tq), mask_ref.dtype)
        return carry

    lax.fori_loop(nkb, seq // sk, zero_body, 0)


def _index_call(qidx, kw, bsz, seq, top_k):
    tq = min(_TQ_IDX, seq)
    nq = seq // tq
    return pl.pallas_call(
        functools.partial(_index_kernel, top_k=top_k),
        grid=(bsz, nq),
        in_specs=[
            pl.BlockSpec((None, IDX_HEADS * IDX_DIM, tq), lambda b, i: (b * nq + i, 0, 0)),
            pl.BlockSpec((tq, LANES), lambda b, i: (b * nq + i, 0)),
            pl.BlockSpec((seq, LANES), lambda b, i: (b, 0)),
        ],
        out_specs=pl.BlockSpec((None, seq, tq), lambda b, i: (b * nq + i, 0, 0)),
        out_shape=jax.ShapeDtypeStruct((bsz * nq, seq, tq), jnp.int8),
        scratch_shapes=[
            pltpu.VMEM((seq, tq), jnp.float32),
            pltpu.VMEM((seq, tq), _COARSE_DTYPE),
        ],
        compiler_params=pltpu.CompilerParams(
            dimension_semantics=("parallel", "parallel"), vmem_limit_bytes=VMEM_LIMIT_BYTES),
        name="index",
    )(qidx, kw, kw)


def _attn_kernel(q_ref, k_ref, vt_ref, mask_ref, o_ref, m_sc, acc_sc, s_sc, bias_sc):
    tq = q_ref.shape[1]
    seq = k_ref.shape[0]
    sk = min(_SK_ATT, seq)
    i = pl.program_id(1)
    nkb = (i * tq + tq + sk - 1) // sk
    last_kb = seq // sk - 1

    m_sc[...] = jnp.full(m_sc.shape, _NEG, jnp.float32)
    acc_sc[...] = jnp.zeros(acc_sc.shape, jnp.float32)

    def stage_bias(kb):
        ks = pl.multiple_of(kb * sk, sk)
        bias_sc[kb % 2] = (1.0 - mask_ref[pl.ds(ks, sk), :].astype(jnp.float32)) * _NEG

    def logits(kb, h):
        ks = pl.multiple_of(kb * sk, sk)
        hs = slice(h * ATTN_HEAD_DIM, (h + 1) * ATTN_HEAD_DIM)
        half = sk // 2
        mx = None
        for part in range(2):
            rs = slice(part * half, (part + 1) * half)
            s = jnp.dot(k_ref[pl.ds(ks + part * half, half), hs], q_ref[hs, :],
                        preferred_element_type=jnp.float32) + bias_sc[kb % 2, rs, :]
            s_sc[h % 2, rs, :] = s
            part_max = jnp.max(s, axis=0, keepdims=True)
            mx = part_max if mx is None else jnp.maximum(mx, part_max)
        return mx

    def body(kb, mx_next):
        ks = pl.multiple_of(kb * sk, sk)
        kb_next = jnp.minimum(kb + 1, last_kb)
        stage_bias(kb_next)
        for h in range(ATTN_HEADS):
            mx = mx_next
            if h + 1 < ATTN_HEADS:
                mx_next = logits(kb, h + 1)
            else:
                mx_next = logits(kb_next, 0)
            m_old = m_sc[h]
            m_new = jnp.maximum(m_old, mx)
            alpha = jnp.exp2(m_old - m_new)
            half = sk // 2
            pv = None
            for part in range(2):
                rs = slice(part * half, (part + 1) * half)
                p = jnp.exp2(s_sc[h % 2, rs, :] - m_new).astype(_MXU_DTYPE)
                part_pv = jnp.dot(vt_ref[kb, h, :, rs], p,
                                  preferred_element_type=jnp.float32)
                pv = part_pv if pv is None else pv + part_pv
            acc_sc[h] = alpha * acc_sc[h] + pv
            m_sc[h] = m_new
        return mx_next

    stage_bias(0)
    lax.fori_loop(0, nkb, body, logits(0, 0))
    for h in range(ATTN_HEADS):
        hs = slice(h * ATTN_HEAD_DIM, (h + 1) * ATTN_HEAD_DIM)
        acc = acc_sc[h]
        out_t = acc[:ATTN_HEAD_DIM, :] / acc[ATTN_HEAD_DIM:ATTN_HEAD_DIM + 1, :]
        o_ref[:, hs] = jnp.transpose(out_t).astype(o_ref.dtype)


def _attn_call(q, k, vt, mask, bsz, seq):
    tq = min(_TQ_ATT, seq)
    nq = seq // tq
    sk = min(_SK_ATT, seq)
    assert vt.shape[-1] == sk and mask.shape[-1] == tq and q.shape[-1] == tq
    return pl.pallas_call(
        _attn_kernel,
        grid=(bsz, nq),
        in_specs=[
            pl.BlockSpec((None, ATTN_WIDTH, tq), lambda b, i: (b * nq + i, 0, 0)),
            pl.BlockSpec((seq, ATTN_WIDTH), lambda b, i: (b, 0)),
            pl.BlockSpec((seq // sk, ATTN_HEADS, _VT_ROWS, sk), lambda b, i: (b, 0, 0, 0)),
            pl.BlockSpec((None, seq, tq), lambda b, i: (b * nq + i, 0, 0)),
        ],
        out_specs=pl.BlockSpec((tq, ATTN_WIDTH), lambda b, i: (b * nq + i, 0)),
        out_shape=jax.ShapeDtypeStruct((bsz * seq, ATTN_WIDTH), _MXU_DTYPE),
        scratch_shapes=[
            pltpu.VMEM((ATTN_HEADS, 1, tq), jnp.float32),
            pltpu.VMEM((ATTN_HEADS, _VT_ROWS, tq), jnp.float32),
            pltpu.VMEM((2, min(_SK_ATT, seq), tq), jnp.float32),
            pltpu.VMEM((2, min(_SK_ATT, seq), tq), jnp.float32),
        ],
        compiler_params=pltpu.CompilerParams(
            dimension_semantics=("parallel", "parallel"), vmem_limit_bytes=VMEM_LIMIT_BYTES),
        name="attn",
    )(q, k, vt, mask)


def _out_kernel(x_ref, ma_ref, gb_ref, b_ref, wb_ref, wo_ref, g1_ref, b1_ref,
                wup_ref, wdn_ref, g2_ref, b2_ref, o_ref, *, alpha):
    branch_b = jnp.dot(b_ref[...], wb_ref[...], preferred_element_type=jnp.float32)
    merged = ma_ref[...] + gb_ref[...] * branch_b
    y = alpha * x_ref[...] + jnp.dot(merged.astype(_MXU_DTYPE), wo_ref[...],
                                     preferred_element_type=jnp.float32)
    x1 = _layer_norm(y, g1_ref[...], b1_ref[...])
    x1b = x1.astype(_MXU_DTYPE)
    acc = jnp.zeros(x1.shape, jnp.float32)
    for c in range(FFN_DIM // _FFN_CHUNK):
        cs = slice(c * _FFN_CHUNK, (c + 1) * _FFN_CHUNK)
        h = jnp.square(jnp.maximum(jnp.dot(x1b, wup_ref[:, cs], preferred_element_type=jnp.float32), 0.0))
        acc = acc + jnp.dot(h.astype(_MXU_DTYPE), wdn_ref[cs, :], preferred_element_type=jnp.float32)
    o_ref[...] = _layer_norm(alpha * x1 + acc, g2_ref[...], b2_ref[...])


def _out_call(x2, ma, gb, battn, w_b, w_o, g1, b1, w_up, w_dn, g2, b2, alpha):
    n = x2.shape[0]
    tm = _TM_OUT
    row_spec = pl.BlockSpec((tm, D_MODEL), lambda i: (i, 0))
    return pl.pallas_call(
        functools.partial(_out_kernel, alpha=alpha),
        grid=(n // tm,),
        in_specs=[row_spec, row_spec, row_spec, row_spec,
                  _resident(w_b.shape), _resident(w_o.shape), _resident(g1.shape), _resident(b1.shape),
                  _resident(w_up.shape), _resident(w_dn.shape), _resident(g2.shape), _resident(b2.shape)],
        out_specs=row_spec,
        out_shape=jax.ShapeDtypeStruct((n, D_MODEL), jnp.float32),
        compiler_params=pltpu.CompilerParams(
            dimension_semantics=("parallel",), vmem_limit_bytes=VMEM_LIMIT_BYTES),
        name="out",
    )(x2, ma, gb, battn, w_b, w_o, g1, b1, w_up, w_dn, g2, b2)


def _pack_w_in(w_in):
    offs = [0]
    for width in IN_SPLITS:
        offs.append(offs[-1] + width)
    u, v, q, k, va, qi, ki, wi, ga, gb = (w_in[:, offs[j]:offs[j + 1]] for j in range(len(IN_SPLITS)))
    pad = jnp.zeros((w_in.shape[0], LANES - IDX_DIM - IDX_HEADS), w_in.dtype)
    packed = jnp.concatenate([u, v, k, ga, gb, ki, wi, pad], axis=1).astype(_MXU_DTYPE)
    transposed = jnp.concatenate([va, q, qi], axis=1).T.astype(_MXU_DTYPE)
    return packed, transposed


def _layer(x, w_in, sgu_ln_g, sgu_ln_b, sgu_w, sgu_b, w_branch_a, w_branch_b, w_out,
           ln1_g, ln1_b, w_ffn_up, w_ffn_down, ln2_g, ln2_b, alpha):
    bsz, seq, _ = x.shape
    assert seq % _TM_PROJ == 0 and seq % _TM_OUT == 0 and _TM_PROJ % CHUNK == 0
    assert seq % min(_TQ_ATT, seq) == 0 and seq % min(_SK_ATT, seq) == 0
    assert seq % min(_TQ_IDX, seq) == 0 and seq % min(_SK_IDX, seq) == 0
    assert _TM_PROJ == min(_TQ_IDX, seq) == min(_TQ_ATT, seq) == min(_SK_ATT, seq)
    top_k = min(IDX_TOPK_MAX, seq // 4)
    x2 = x.reshape(bsz * seq, D_MODEL)
    vec = lambda a: a.reshape(1, -1)
    cast = lambda a: a.astype(_MXU_DTYPE)

    w_packed, w_t = _pack_w_in(w_in)
    q, k, vt, qidx, kw, ma, gb = _proj_call(
        x2, w_packed, w_t, vec(sgu_ln_g), vec(sgu_ln_b), sgu_w, sgu_b.T, cast(w_branch_a))
    mask = _index_call(qidx, kw, bsz, seq, top_k)
    battn = _attn_call(q, k, vt, mask, bsz, seq)
    out = _out_call(x2, ma, gb, battn, cast(w_branch_b), cast(w_out), vec(ln1_g), vec(ln1_b),
                    cast(w_ffn_up), cast(w_ffn_down), vec(ln2_g), vec(ln2_b), alpha)
    return out.reshape(bsz, seq, D_MODEL)


def kernel(x, w_in, sgu_ln_g, sgu_ln_b, sgu_w, sgu_b, w_branch_a, w_branch_b, w_out,
           ln1_g, ln1_b, w_ffn_up, w_ffn_down, ln2_g, ln2_b):
    depth = w_in.shape[0]
    alpha = (2.0 * depth) ** 0.25
    for i in range(depth):
        x = _layer(x, w_in[i], sgu_ln_g[i], sgu_ln_b[i], sgu_w[i], sgu_b[i], w_branch_a[i],
                   w_branch_b[i], w_out[i], ln1_g[i], ln1_b[i], w_ffn_up[i], w_ffn_down[i],
                   ln2_g[i], ln2_b[i], alpha)
    return x
```

```python
import functools

import jax
import jax.numpy as jnp
from jax import lax
from jax.experimental import pallas as pl
from jax.experimental.pallas import tpu as pltpu

D_MODEL = 1024
CHUNK = 128
SGU_GROUPS = 8
SGU_GROUP_DIM = D_MODEL // SGU_GROUPS
ATTN_HEADS = 8
ATTN_HEAD_DIM = 128
ATTN_WIDTH = ATTN_HEADS * ATTN_HEAD_DIM
IDX_HEADS = 8
IDX_DIM = 64
IDX_TOPK_MAX = 256
FFN_DIM = 4 * D_MODEL
LN_EPS = 1e-5
IN_SPLITS = (D_MODEL, D_MODEL, ATTN_WIDTH, ATTN_WIDTH, ATTN_WIDTH,
             IDX_HEADS * IDX_DIM, IDX_DIM, IDX_HEADS, D_MODEL, D_MODEL)

LANES = 128
VMEM_LIMIT_BYTES = 56 * 2**20

_MXU_DTYPE = jnp.bfloat16
_NEG = -1e30
_LOG2_E = 1.4426950408889634

_C_U, _C_V, _C_K, _C_GA, _C_GB = (i * D_MODEL for i in range(5))
_C_KW = 5 * D_MODEL
_R_V, _R_Q, _R_QI, _R_END = 0, ATTN_WIDTH, 2 * ATTN_WIDTH, 2 * ATTN_WIDTH + IDX_HEADS * IDX_DIM

_TM_PROJ = 512
_TQ_IDX = 512
_SK_IDX = 512
_ROWS_IDX = 32
_TQ_ATT = 512
_SK_ATT = 512
_TM_OUT = 512
_FFN_CHUNK = 1024

_NT_DIMS = (((1,), (1,)), ((), ()))
_VT_ROWS = ATTN_HEAD_DIM + 16


def _layer_norm(x, g, b):
    mu = jnp.mean(x, axis=-1, keepdims=True)
    xc = x - mu
    var = jnp.mean(xc * xc, axis=-1, keepdims=True)
    return xc * lax.rsqrt(var + LN_EPS) * g + b


def _resident(shape):
    nd = len(shape)
    return pl.BlockSpec(shape, lambda *_: (0,) * nd, pipeline_mode=pl.Buffered(1))


def _proj_kernel(x_ref, w_ref, wt_ref, lng_ref, lnb_ref, sw_ref, sbt_ref, wa_ref,
                 qt_ref, k_ref, vt_ref, qit_ref, kw_ref, ma_ref, gb_ref,
                 u_sc, vn_sc, a_sc):
    tm = x_ref.shape[0]
    xb = x_ref[...].astype(_MXU_DTYPE)

    def seg(lo, width):
        return jnp.dot(xb, w_ref[:, lo:lo + width], preferred_element_type=jnp.float32)

    def seg_t(lo, hi):
        return lax.dot_general(wt_ref[lo:hi, :], xb, _NT_DIMS, preferred_element_type=jnp.float32)

    k_ref[...] = seg(_C_K, ATTN_WIDTH).astype(k_ref.dtype)
    qt_ref[...] = (seg_t(_R_Q, _R_QI) * (ATTN_HEAD_DIM ** -0.5 * _LOG2_E)).astype(qt_ref.dtype)
    qit_ref[...] = seg_t(_R_QI, _R_END).astype(qit_ref.dtype)
    vt = seg_t(_R_V, _R_Q)
    for h in range(ATTN_HEADS):
        vt_ref[h, :ATTN_HEAD_DIM, :] = vt[h * ATTN_HEAD_DIM:(h + 1) * ATTN_HEAD_DIM, :].astype(vt_ref.dtype)
        vt_ref[h, ATTN_HEAD_DIM:, :] = jnp.ones((_VT_ROWS - ATTN_HEAD_DIM, tm), vt_ref.dtype)
    kw_ref[...] = seg(_C_KW, LANES)
    gb_ref[...] = jax.nn.sigmoid(seg(_C_GB, D_MODEL))

    u_sc[...] = jax.nn.gelu(seg(_C_U, D_MODEL))
    vn = _layer_norm(jax.nn.gelu(seg(_C_V, D_MODEL)), lng_ref[...], lnb_ref[...])
    vn_sc[...] = vn.astype(vn_sc.dtype)
    row = lax.broadcasted_iota(jnp.int32, (CHUNK, CHUNK), 0)
    col = lax.broadcasted_iota(jnp.int32, (CHUNK, CHUNK), 1)
    causal = col <= row
    for g in range(SGU_GROUPS):
        wm = jnp.where(causal, sw_ref[g], 0.0).astype(_MXU_DTYPE)
        bias = sbt_ref[:, g:g + 1]
        cs = slice(g * SGU_GROUP_DIM, (g + 1) * SGU_GROUP_DIM)
        for c in range(tm // CHUNK):
            rs = slice(c * CHUNK, (c + 1) * CHUNK)
            s = jnp.dot(wm, vn_sc[rs, cs], preferred_element_type=jnp.float32) + bias
            a_sc[rs, cs] = (u_sc[rs, cs] * s).astype(a_sc.dtype)

    branch_a = jnp.dot(a_sc[...], wa_ref[...], preferred_element_type=jnp.float32)
    ma_ref[...] = jax.nn.sigmoid(seg(_C_GA, D_MODEL)) * branch_a


def _proj_call(x2, w_packed, w_t, ln_g, ln_b, sgu_w, sgu_bt, w_a):
    n = x2.shape[0]
    tm = _TM_PROJ
    row_spec = lambda width: pl.BlockSpec((tm, width), lambda i: (i, 0))
    out_shapes = (
        jax.ShapeDtypeStruct((n // tm, ATTN_WIDTH, tm), _MXU_DTYPE),
        jax.ShapeDtypeStruct((n, ATTN_WIDTH), _MXU_DTYPE),
        jax.ShapeDtypeStruct((n // tm, ATTN_HEADS, _VT_ROWS, tm), _MXU_DTYPE),
        jax.ShapeDtypeStruct((n // tm, IDX_HEADS * IDX_DIM, tm), _MXU_DTYPE),
        jax.ShapeDtypeStruct((n, LANES), jnp.float32),
        jax.ShapeDtypeStruct((n, D_MODEL), jnp.float32),
        jax.ShapeDtypeStruct((n, D_MODEL), jnp.float32),
    )
    vt_spec = pl.BlockSpec((None, ATTN_HEADS, _VT_ROWS, tm), lambda i: (i, 0, 0, 0))
    slab_spec = lambda rows: pl.BlockSpec((None, rows, tm), lambda i: (i, 0, 0))
    return pl.pallas_call(
        _proj_kernel,
        grid=(n // tm,),
        in_specs=[
            row_spec(D_MODEL),
            _resident(w_packed.shape),
            _resident(w_t.shape),
            _resident(ln_g.shape),
            _resident(ln_b.shape),
            _resident(sgu_w.shape),
            _resident(sgu_bt.shape),
            _resident(w_a.shape),
        ],
        out_specs=[slab_spec(ATTN_WIDTH), row_spec(ATTN_WIDTH), vt_spec,
                   slab_spec(IDX_HEADS * IDX_DIM), row_spec(LANES), row_spec(D_MODEL), row_spec(D_MODEL)],
        out_shape=out_shapes,
        scratch_shapes=[
            pltpu.VMEM((tm, D_MODEL), jnp.float32),
            pltpu.VMEM((tm, D_MODEL), _MXU_DTYPE),
            pltpu.VMEM((tm, D_MODEL), _MXU_DTYPE),
        ],
        compiler_params=pltpu.CompilerParams(
            dimension_semantics=("parallel",), vmem_limit_bytes=VMEM_LIMIT_BYTES),
        name="proj",
    )(x2, w_packed, w_t, ln_g, ln_b, sgu_w, sgu_bt, w_a)


def _key_to_f32(key):
    bits = jnp.where(key < 0, key & jnp.int32(0x7FFFFFFF), ~key)
    return lax.bitcast_convert_type(bits, jnp.float32)


_KEY16_SIGN = 1 << 15
_KEY16_MAX = (1 << 16) - 1


def _key16_to_f32(key16):
    pattern = jnp.where(key16 >= _KEY16_SIGN, key16 - _KEY16_SIGN, _KEY16_MAX - key16)
    return lax.bitcast_convert_type(lax.shift_left(pattern, 16), jnp.float32)


_I32_MIN = -2 ** 31
_COARSE_DTYPE = jnp.bfloat16
_BF16_KEY_STEP = 1 << 16
_FINE_STEPS = 17
_FINE_ALWAYS = 5
_FINE_GROUP = 4


def _select_bias(qit_ref, kwq_ref, kwk_ref, sc_ref, hb_ref, top_k):
    tq = qit_ref.shape[1]
    seq = kwk_ref.shape[0]
    sk = min(_SK_IDX, seq)
    rows = _ROWS_IDX
    i = pl.program_id(1)
    col0 = i * tq
    nkb = (col0 + tq + sk - 1) // sk
    t_row = col0 + lax.broadcasted_iota(jnp.int32, (1, tq), 1)
    w8 = jnp.transpose(kwq_ref[...])[IDX_DIM:IDX_DIM + IDX_HEADS, :] * (IDX_HEADS ** -0.5 * IDX_DIM ** -0.5)
    lane = lax.broadcasted_iota(jnp.int32, (sk, LANES), 1)

    def score_body(kb, carry):
        ks = pl.multiple_of(kb * sk, sk)
        kblk = jnp.where(lane < IDX_DIM, kwk_ref[pl.ds(ks, sk), :], 0.0)
        k_lo = kblk.astype(_MXU_DTYPE)
        k_hi = pltpu.roll(kblk, IDX_DIM, axis=1).astype(_MXU_DTYPE)
        blk = sc_ref.at[pl.ds(ks, sk), :]
        for hp in range(IDX_HEADS // 2):
            qpair = qit_ref[hp * LANES:(hp + 1) * LANES, :]
            for half, kk in enumerate((k_lo, k_hi)):
                h = 2 * hp + half
                logits = jnp.dot(kk, qpair, preferred_element_type=jnp.float32)
                term = w8[h:h + 1, :] * jnp.maximum(logits, 0.0)
                if h == 0:
                    blk[...] = term
                elif h < IDX_HEADS - 1:
                    blk[...] = blk[...] + term
                else:
                    s_col = ks + lax.broadcasted_iota(jnp.int32, (sk, tq), 0)
                    score = jnp.where(s_col <= t_row, blk[...] + term, -jnp.inf)
                    blk[...] = score
                    hb_ref[pl.ds(ks, sk), :] = score.astype(_COARSE_DTYPE)
        return carry

    lax.fori_loop(0, nkb, score_body, 0)

    def for_offsets(fn, init):
        def body(kb, carry):
            ks = pl.multiple_of(kb * sk, sk)
            for j in range(sk // rows):
                carry = fn(carry, pl.multiple_of(ks + j * rows, rows))
            return carry
        return lax.fori_loop(0, nkb, body, init)

    def for_tiles(fn, init):
        return for_offsets(lambda carry, off: fn(carry, sc_ref[pl.ds(off, rows), :], off), init)

    def count(indicator):
        acc = for_tiles(lambda acc, tile, off: acc + indicator(tile, off), jnp.zeros((rows, tq), jnp.float32))
        return jnp.sum(acc, axis=0, keepdims=True)

    keep_all = t_row < top_k

    def coarse_step(it, key16):
        trial = key16 | lax.shift_left(jnp.int32(1), 15 - it)
        cand = jnp.broadcast_to(_key16_to_f32(trial), (rows, tq)).astype(_COARSE_DTYPE)
        def add_tile(acc, off):
            ge = hb_ref[pl.ds(off, rows), :] >= cand
            return acc + jnp.where(ge, _COARSE_DTYPE(1), _COARSE_DTYPE(0))
        acc = for_offsets(add_tile, jnp.zeros((rows, tq), _COARSE_DTYPE))
        cnt = jnp.sum(acc.astype(jnp.float32), axis=0, keepdims=True)
        return jnp.where(cnt >= top_k, trial, key16)

    key16 = lax.fori_loop(0, 16, coarse_step, jnp.zeros((1, tq), jnp.int32))
    t16_bits = lax.bitcast_convert_type(_key16_to_f32(key16), jnp.int32)
    t16_key = jnp.where(t16_bits < 0, ~t16_bits, t16_bits | jnp.int32(_I32_MIN))
    key_base = t16_key - _BF16_KEY_STEP

    def fine_step(it, carry):
        offset, cnt_key = carry
        trial = offset | lax.shift_left(jnp.int32(1), _FINE_STEPS - 1 - it)
        cand = jnp.broadcast_to(_key_to_f32(key_base + trial), (rows, tq))
        cnt = count(lambda tile, off: jnp.where(tile >= cand, 1.0, 0.0))
        ok = cnt >= top_k
        return jnp.where(ok, trial, offset), jnp.where(ok, cnt, cnt_key)

    state = (jnp.zeros((1, tq), jnp.int32), jnp.full((1, tq), float(seq), jnp.float32))
    state = lax.fori_loop(0, _FINE_ALWAYS, fine_step, state)
    for start in range(_FINE_ALWAYS, _FINE_STEPS, _FINE_GROUP):
        stop = min(start + _FINE_GROUP, _FINE_STEPS)
        unresolved = jnp.where(keep_all, 0.0, jnp.where(state[1] > top_k, 1.0, 0.0))
        state = lax.cond(
            jnp.max(unresolved) > 0.0,
            lambda st, start=start, stop=stop: lax.fori_loop(start, stop, fine_step, st),
            lambda st: st,
            state)
    offset, n_ge = state
    tau = jnp.where(keep_all, -jnp.inf, _key_to_f32(key_base + offset))
    tau_b = jnp.broadcast_to(tau, (rows, tq))

    excess = jnp.where(keep_all, 0.0, jnp.where(n_ge > top_k, 1.0, 0.0))
    sub_row = lax.broadcasted_iota(jnp.int32, (rows, tq), 0)
    t_b = jnp.broadcast_to(t_row, (rows, tq))

    def write_with_ties():
        def mark_tied(acc, tile, off):
            eq = jnp.where(tile == tau_b, 1.0, 0.0)
            hb_ref[pl.ds(off, rows), :] = eq.astype(_COARSE_DTYPE)
            return acc + eq
        n_eq = jnp.sum(for_tiles(mark_tied, jnp.zeros((rows, tq), jnp.float32)), axis=0, keepdims=True)
        need = top_k - (n_ge - n_eq)
        r = lax.broadcasted_iota(jnp.int32, (sk, sk), 0)
        c = lax.broadcasted_iota(jnp.int32, (sk, sk), 1)
        tri = jnp.where(c <= r, 1.0, 0.0).astype(_COARSE_DTYPE)
        tau_blk = jnp.broadcast_to(tau, (sk, tq))
        def block(kb, taken):
            ks = pl.multiple_of(kb * sk, sk)
            rank = taken + jnp.dot(tri, hb_ref[pl.ds(ks, sk), :], preferred_element_type=jnp.float32)
            tile = sc_ref[pl.ds(ks, sk), :]
            tied = jnp.where(tile == tau_blk, jnp.where(rank <= need, 0.0, _NEG), _NEG)
            s_col = ks + lax.broadcasted_iota(jnp.int32, (sk, tq), 0)
            sc_ref[pl.ds(ks, sk), :] = jnp.where(s_col <= t_row, jnp.where(tile > tau_blk, 0.0, tied), _NEG)
            return rank[sk - 1:sk, :]
        lax.fori_loop(0, nkb, block, jnp.zeros((1, tq), jnp.float32))

    def write_without_ties():
        def write_bias(carry, tile, off):
            sc_ref[pl.ds(off, rows), :] = jnp.where(
                off + sub_row <= t_b, jnp.where(tile >= tau_b, 0.0, _NEG), _NEG)
            return carry
        for_tiles(write_bias, 0)

    lax.cond(jnp.max(excess) > 0.0, write_with_ties, write_without_ties)


def _attend(q_ref, k_ref, vt_ref, bias_ref, o_ref, m_sc, acc_sc, s_sc):
    tq = q_ref.shape[1]
    seq = k_ref.shape[0]
    sk = min(_SK_ATT, seq)
    i = pl.program_id(1)
    nkb = (i * tq + tq + sk - 1) // sk

    m_sc[...] = jnp.full(m_sc.shape, _NEG, jnp.float32)
    acc_sc[...] = jnp.zeros(acc_sc.shape, jnp.float32)

    def logits(kb, h):
        ks = pl.multiple_of(kb * sk, sk)
        hs = slice(h * ATTN_HEAD_DIM, (h + 1) * ATTN_HEAD_DIM)
        s = jnp.dot(k_ref[pl.ds(ks, sk), hs], q_ref[hs, :],
                    preferred_element_type=jnp.float32) + bias_ref[pl.ds(ks, sk), :]
        s_sc[h % 2] = s
        return jnp.max(s, axis=0, keepdims=True)

    def body(kb, mx_next):
        ks = pl.multiple_of(kb * sk, sk)
        kb_next = jnp.minimum(kb + 1, nkb - 1)
        for h in range(ATTN_HEADS):
            mx = mx_next
            if h + 1 < ATTN_HEADS:
                mx_next = logits(kb, h + 1)
            else:
                mx_next = logits(kb_next, 0)
            m_old = m_sc[h]
            m_new = jnp.maximum(m_old, mx)
            alpha = jnp.exp2(m_old - m_new)
            p = jnp.exp2(s_sc[h % 2] - m_new).astype(_MXU_DTYPE)
            acc_sc[h] = alpha * acc_sc[h] + jnp.dot(vt_ref[kb, h], p, preferred_element_type=jnp.float32)
            m_sc[h] = m_new
        return mx_next

    lax.fori_loop(0, nkb, body, logits(0, 0))
    for h in range(ATTN_HEADS):
        hs = slice(h * ATTN_HEAD_DIM, (h + 1) * ATTN_HEAD_DIM)
        acc = acc_sc[h]
        out_t = acc[:ATTN_HEAD_DIM, :] / acc[ATTN_HEAD_DIM:ATTN_HEAD_DIM + 1, :]
        o_ref[:, hs] = jnp.transpose(out_t).astype(o_ref.dtype)


def _sparse_attn_kernel(qit_ref, kwq_ref, kwk_ref, q_ref, k_ref, vt_ref, o_ref,
                        sc_ref, hb_ref, m_sc, acc_sc, s_sc, *, top_k):
    _select_bias(qit_ref, kwq_ref, kwk_ref, sc_ref, hb_ref, top_k)
    _attend(q_ref, k_ref, vt_ref, sc_ref, o_ref, m_sc, acc_sc, s_sc)


def _sparse_attn_call(qidx, kw, q, k, vt, bsz, seq, top_k):
    tq = min(_TQ_ATT, seq)
    nq = seq // tq
    sk = min(_SK_ATT, seq)
    assert vt.shape[-1] == sk and q.shape[-1] == tq and qidx.shape[-1] == tq
    per_batch = lambda shape: pl.BlockSpec(shape, lambda b, i: (b,) + (0,) * (len(shape) - 1),
                                           pipeline_mode=pl.Buffered(1))
    return pl.pallas_call(
        functools.partial(_sparse_attn_kernel, top_k=top_k),
        grid=(bsz, nq),
        in_specs=[
            pl.BlockSpec((None, IDX_HEADS * IDX_DIM, tq), lambda b, i: (b * nq + i, 0, 0)),
            pl.BlockSpec((tq, LANES), lambda b, i: (b * nq + i, 0)),
            per_batch((seq, LANES)),
            pl.BlockSpec((None, ATTN_WIDTH, tq), lambda b, i: (b * nq + i, 0, 0)),
            per_batch((seq, ATTN_WIDTH)),
            per_batch((seq // sk, ATTN_HEADS, _VT_ROWS, sk)),
        ],
        out_specs=pl.BlockSpec((tq, ATTN_WIDTH), lambda b, i: (b * nq + i, 0)),
        out_shape=jax.ShapeDtypeStruct((bsz * seq, ATTN_WIDTH), _MXU_DTYPE),
        scratch_shapes=[
            pltpu.VMEM((seq, tq), jnp.float32),
            pltpu.VMEM((seq, tq), _COARSE_DTYPE),
            pltpu.VMEM((ATTN_HEADS, 1, tq), jnp.float32),
            pltpu.VMEM((ATTN_HEADS, _VT_ROWS, tq), jnp.float32),
            pltpu.VMEM((2, sk, tq), jnp.float32),
        ],
        compiler_params=pltpu.CompilerParams(
            dimension_semantics=("parallel", "parallel"), vmem_limit_bytes=VMEM_LIMIT_BYTES),
        name="sparse_attn",
    )(qidx, kw, kw, q, k, vt)


def _out_kernel(x_ref, ma_ref, gb_ref, b_ref, wb_ref, wo_ref, g1_ref, b1_ref,
                wup_ref, wdn_ref, g2_ref, b2_ref, o_ref, *, alpha):
    branch_b = jnp.dot(b_ref[...], wb_ref[...], preferred_element_type=jnp.float32)
    merged = ma_ref[...] + gb_ref[...] * branch_b
    y = alpha * x_ref[...] + jnp.dot(merged.astype(_MXU_DTYPE), wo_ref[...],
                                     preferred_element_type=jnp.float32)
    x1 = _layer_norm(y, g1_ref[...], b1_ref[...])
    x1b = x1.astype(_MXU_DTYPE)
    acc = jnp.zeros(x1.shape, jnp.float32)
    for c in range(FFN_DIM // _FFN_CHUNK):
        cs = slice(c * _FFN_CHUNK, (c + 1) * _FFN_CHUNK)
        h = jnp.square(jnp.maximum(jnp.dot(x1b, wup_ref[:, cs], preferred_element_type=jnp.float32), 0.0))
        acc = acc + jnp.dot(h.astype(_MXU_DTYPE), wdn_ref[cs, :], preferred_element_type=jnp.float32)
    o_ref[...] = _layer_norm(alpha * x1 + acc, g2_ref[...], b2_ref[...])


def _out_call(x2, ma, gb, battn, w_b, w_o, g1, b1, w_up, w_dn, g2, b2, alpha):
    n = x2.shape[0]
    tm = _TM_OUT
    row_spec = pl.BlockSpec((tm, D_MODEL), lambda i: (i, 0))
    return pl.pallas_call(
        functools.partial(_out_kernel, alpha=alpha),
        grid=(n // tm,),
        in_specs=[row_spec, row_spec, row_spec, row_spec,
                  _resident(w_b.shape), _resident(w_o.shape), _resident(g1.shape), _resident(b1.shape),
                  _resident(w_up.shape), _resident(w_dn.shape), _resident(g2.shape), _resident(b2.shape)],
        out_specs=row_spec,
        out_shape=jax.ShapeDtypeStruct((n, D_MODEL), jnp.float32),
        compiler_params=pltpu.CompilerParams(
            dimension_semantics=("parallel",), vmem_limit_bytes=VMEM_LIMIT_BYTES),
        name="out",
    )(x2, ma, gb, battn, w_b, w_o, g1, b1, w_up, w_dn, g2, b2)


def _pack_w_in(w_in):
    offs = [0]
    for width in IN_SPLITS:
        offs.append(offs[-1] + width)
    u, v, q, k, va, qi, ki, wi, ga, gb = (w_in[:, offs[j]:offs[j + 1]] for j in range(len(IN_SPLITS)))
    pad = jnp.zeros((w_in.shape[0], LANES - IDX_DIM - IDX_HEADS), w_in.dtype)
    packed = jnp.concatenate([u, v, k, ga, gb, ki, wi, pad], axis=1).astype(_MXU_DTYPE)
    transposed = jnp.concatenate([va, q, qi], axis=1).T.astype(_MXU_DTYPE)
    return packed, transposed


def _layer(x, w_in, sgu_ln_g, sgu_ln_b, sgu_w, sgu_b, w_branch_a, w_branch_b, w_out,
           ln1_g, ln1_b, w_ffn_up, w_ffn_down, ln2_g, ln2_b, alpha):
    bsz, seq, _ = x.shape
    assert seq % _TM_PROJ == 0 and seq % _TM_OUT == 0 and _TM_PROJ % CHUNK == 0
    assert seq % min(_TQ_ATT, seq) == 0 and seq % min(_SK_ATT, seq) == 0
    assert seq % min(_TQ_IDX, seq) == 0 and seq % min(_SK_IDX, seq) == 0
    assert _TM_PROJ == min(_TQ_IDX, seq) == min(_TQ_ATT, seq) == min(_SK_ATT, seq)
    top_k = min(IDX_TOPK_MAX, seq // 4)
    x2 = x.reshape(bsz * seq, D_MODEL)
    vec = lambda a: a.reshape(1, -1)
    cast = lambda a: a.astype(_MXU_DTYPE)

    w_packed, w_t = _pack_w_in(w_in)
    q, k, vt, qidx, kw, ma, gb = _proj_call(
        x2, w_packed, w_t, vec(sgu_ln_g), vec(sgu_ln_b), sgu_w, sgu_b.T, cast(w_branch_a))
    battn = _sparse_attn_call(qidx, kw, q, k, vt, bsz, seq, top_k)
    out = _out_call(x2, ma, gb, battn, cast(w_branch_b), cast(w_out), vec(ln1_g), vec(ln1_b),
                    cast(w_ffn_up), cast(w_ffn_down), vec(ln2_g), vec(ln2_b), alpha)
    return out.reshape(bsz, seq, D_MODEL)


def kernel(x, w_in, sgu_ln_g, sgu_ln_b, sgu_w, sgu_b, w_branch_a, w_branch_b, w_out,
           ln1_g, ln1_b, w_ffn_up, w_ffn_down, ln2_g, ln2_b):
    depth = w_in.shape[0]
    alpha = (2.0 * depth) ** 0.25
    for i in range(depth):
        x = _layer(x, w_in[i], sgu_ln_g[i], sgu_ln_b[i], sgu_w[i], sgu_b[i], w_branch_a[i],
                   w_branch_b[i], w_out[i], ln1_g[i], ln1_b[i], w_ffn_up[i], w_ffn_down[i],
                   ln2_g[i], ln2_b[i], alpha)
    return x
```

```python
import functools

import jax
import jax.numpy as jnp
from jax import lax
from jax.experimental import pallas as pl
from jax.experimental.pallas import tpu as pltpu

D_MODEL = 1024
CHUNK = 128
SGU_GROUPS = 8
SGU_GROUP_DIM = D_MODEL // SGU_GROUPS
ATTN_HEADS = 8
ATTN_HEAD_DIM = 128
ATTN_WIDTH = ATTN_HEADS * ATTN_HEAD_DIM
IDX_HEADS = 8
IDX_DIM = 64
IDX_TOPK_MAX = 256
FFN_DIM = 4 * D_MODEL
LN_EPS = 1e-5
IN_SPLITS = (D_MODEL, D_MODEL, ATTN_WIDTH, ATTN_WIDTH, ATTN_WIDTH,
             IDX_HEADS * IDX_DIM, IDX_DIM, IDX_HEADS, D_MODEL, D_MODEL)

LANES = 128
VMEM_LIMIT_BYTES = 56 * 2**20

_MXU_DTYPE = jnp.bfloat16
_NEG = -1e30
_LOG2_E = 1.4426950408889634

_C_U, _C_V, _C_K, _C_GA, _C_GB = (i * D_MODEL for i in range(5))
_C_KW = 5 * D_MODEL
_R_V, _R_Q, _R_QI, _R_END = 0, ATTN_WIDTH, 2 * ATTN_WIDTH, 2 * ATTN_WIDTH + IDX_HEADS * IDX_DIM

_TM_PROJ = 512
_TQ_IDX = 512
_SK_IDX = 512
_ROWS_IDX = 32
_TQ_ATT = 512
_SK_ATT = 512
_TM_OUT = 512
_FFN_CHUNK = 1024

_NT_DIMS = (((1,), (1,)), ((), ()))
_VT_ROWS = ATTN_HEAD_DIM + 16


def _layer_norm(x, g, b):
    mu = jnp.mean(x, axis=-1, keepdims=True)
    xc = x - mu
    var = jnp.mean(xc * xc, axis=-1, keepdims=True)
    return xc * lax.rsqrt(var + LN_EPS) * g + b


def _resident(shape):
    nd = len(shape)
    return pl.BlockSpec(shape, lambda *_: (0,) * nd, pipeline_mode=pl.Buffered(1))


def _proj_kernel(x_ref, w_ref, wt_ref, lng_ref, lnb_ref, sw_ref, sbt_ref, wa_ref,
                 qt_ref, k_ref, vt_ref, qit_ref, kw_ref, ma_ref, gb_ref,
                 u_sc, vn_sc, a_sc):
    tm = x_ref.shape[0]
    xb = x_ref[...].astype(_MXU_DTYPE)

    def seg(lo, width):
        return jnp.dot(xb, w_ref[:, lo:lo + width], preferred_element_type=jnp.float32)

    def seg_t(lo, hi):
        return lax.dot_general(wt_ref[lo:hi, :], xb, _NT_DIMS, preferred_element_type=jnp.float32)

    k_ref[...] = seg(_C_K, ATTN_WIDTH).astype(k_ref.dtype)
    qt_ref[...] = (seg_t(_R_Q, _R_QI) * (ATTN_HEAD_DIM ** -0.5 * _LOG2_E)).astype(qt_ref.dtype)
    qit_ref[...] = seg_t(_R_QI, _R_END).astype(qit_ref.dtype)
    vt = seg_t(_R_V, _R_Q)
    for h in range(ATTN_HEADS):
        vt_ref[h, :ATTN_HEAD_DIM, :] = vt[h * ATTN_HEAD_DIM:(h + 1) * ATTN_HEAD_DIM, :].astype(vt_ref.dtype)
        vt_ref[h, ATTN_HEAD_DIM:, :] = jnp.ones((_VT_ROWS - ATTN_HEAD_DIM, tm), vt_ref.dtype)
    kw_ref[...] = seg(_C_KW, LANES)
    gb_ref[...] = jax.nn.sigmoid(seg(_C_GB, D_MODEL))

    u_sc[...] = jax.nn.gelu(seg(_C_U, D_MODEL))
    vn = _layer_norm(jax.nn.gelu(seg(_C_V, D_MODEL)), lng_ref[...], lnb_ref[...])
    vn_sc[...] = vn.astype(vn_sc.dtype)
    row = lax.broadcasted_iota(jnp.int32, (CHUNK, CHUNK), 0)
    col = lax.broadcasted_iota(jnp.int32, (CHUNK, CHUNK), 1)
    causal = col <= row
    for g in range(SGU_GROUPS):
        wm = jnp.where(causal, sw_ref[g], 0.0).astype(_MXU_DTYPE)
        bias = sbt_ref[:, g:g + 1]
        cs = slice(g * SGU_GROUP_DIM, (g + 1) * SGU_GROUP_DIM)
        for c in range(tm // CHUNK):
            rs = slice(c * CHUNK, (c + 1) * CHUNK)
            s = jnp.dot(wm, vn_sc[rs, cs], preferred_element_type=jnp.float32) + bias
            a_sc[rs, cs] = (u_sc[rs, cs] * s).astype(a_sc.dtype)

    branch_a = jnp.dot(a_sc[...], wa_ref[...], preferred_element_type=jnp.float32)
    ma_ref[...] = jax.nn.sigmoid(seg(_C_GA, D_MODEL)) * branch_a


def _proj_call(x2, w_packed, w_t, ln_g, ln_b, sgu_w, sgu_bt, w_a):
    n = x2.shape[0]
    tm = _TM_PROJ
    row_spec = lambda width: pl.BlockSpec((tm, width), lambda i: (i, 0))
    out_shapes = (
        jax.ShapeDtypeStruct((n // tm, ATTN_WIDTH, tm), _MXU_DTYPE),
        jax.ShapeDtypeStruct((n, ATTN_WIDTH), _MXU_DTYPE),
        jax.ShapeDtypeStruct((n // tm, ATTN_HEADS, _VT_ROWS, tm), _MXU_DTYPE),
        jax.ShapeDtypeStruct((n // tm, IDX_HEADS * IDX_DIM, tm), _MXU_DTYPE),
        jax.ShapeDtypeStruct((n, LANES), jnp.float32),
        jax.ShapeDtypeStruct((n, D_MODEL), jnp.float32),
        jax.ShapeDtypeStruct((n, D_MODEL), jnp.float32),
    )
    vt_spec = pl.BlockSpec((None, ATTN_HEADS, _VT_ROWS, tm), lambda i: (i, 0, 0, 0))
    slab_spec = lambda rows: pl.BlockSpec((None, rows, tm), lambda i: (i, 0, 0))
    return pl.pallas_call(
        _proj_kernel,
        grid=(n // tm,),
        in_specs=[
            row_spec(D_MODEL),
            _resident(w_packed.shape),
            _resident(w_t.shape),
            _resident(ln_g.shape),
            _resident(ln_b.shape),
            _resident(sgu_w.shape),
            _resident(sgu_bt.shape),
            _resident(w_a.shape),
        ],
        out_specs=[slab_spec(ATTN_WIDTH), row_spec(ATTN_WIDTH), vt_spec,
                   slab_spec(IDX_HEADS * IDX_DIM), row_spec(LANES), row_spec(D_MODEL), row_spec(D_MODEL)],
        out_shape=out_shapes,
        scratch_shapes=[
            pltpu.VMEM((tm, D_MODEL), jnp.float32),
            pltpu.VMEM((tm, D_MODEL), _MXU_DTYPE),
            pltpu.VMEM((tm, D_MODEL), _MXU_DTYPE),
        ],
        compiler_params=pltpu.CompilerParams(
            dimension_semantics=("parallel",), vmem_limit_bytes=VMEM_LIMIT_BYTES),
        name="proj",
    )(x2, w_packed, w_t, ln_g, ln_b, sgu_w, sgu_bt, w_a)


def _key_to_f32(key):
    bits = jnp.where(key < 0, key & jnp.int32(0x7FFFFFFF), ~key)
    return lax.bitcast_convert_type(bits, jnp.float32)


_KEY16_SIGN = 1 << 15
_KEY16_MAX = (1 << 16) - 1


def _key16_to_f32(key16):
    pattern = jnp.where(key16 >= _KEY16_SIGN, key16 - _KEY16_SIGN, _KEY16_MAX - key16)
    return lax.bitcast_convert_type(lax.shift_left(pattern, 16), jnp.float32)


_I32_MIN = -2 ** 31
_COARSE_DTYPE = jnp.bfloat16
_BF16_KEY_STEP = 1 << 16
_FINE_STEPS = 17
_FINE_ALWAYS = 5
_FINE_GROUP = 4


def _select_bias(qit_ref, kwq_ref, kwk_ref, sc_ref, hb_ref, top_k):
    tq = qit_ref.shape[1]
    seq = kwk_ref.shape[0]
    sk = min(_SK_IDX, seq)
    rows = _ROWS_IDX
    i = pl.program_id(1)
    col0 = i * tq
    nkb = (col0 + tq + sk - 1) // sk
    t_row = col0 + lax.broadcasted_iota(jnp.int32, (1, tq), 1)
    w8 = jnp.transpose(kwq_ref[...])[IDX_DIM:IDX_DIM + IDX_HEADS, :] * (IDX_HEADS ** -0.5 * IDX_DIM ** -0.5)
    lane = lax.broadcasted_iota(jnp.int32, (sk, LANES), 1)

    def score_body(kb, carry):
        ks = pl.multiple_of(kb * sk, sk)
        kblk = jnp.where(lane < IDX_DIM, kwk_ref[pl.ds(ks, sk), :], 0.0)
        k_lo = kblk.astype(_MXU_DTYPE)
        k_hi = pltpu.roll(kblk, IDX_DIM, axis=1).astype(_MXU_DTYPE)
        blk = sc_ref.at[pl.ds(ks, sk), :]
        for hp in range(IDX_HEADS // 2):
            qpair = qit_ref[hp * LANES:(hp + 1) * LANES, :]
            for half, kk in enumerate((k_lo, k_hi)):
                h = 2 * hp + half
                logits = jnp.dot(kk, qpair, preferred_element_type=jnp.float32)
                term = w8[h:h + 1, :] * jnp.maximum(logits, 0.0)
                if h == 0:
                    blk[...] = term
                elif h < IDX_HEADS - 1:
                    blk[...] = blk[...] + term
                else:
                    s_col = ks + lax.broadcasted_iota(jnp.int32, (sk, tq), 0)
                    score = jnp.where(s_col <= t_row, blk[...] + term, -jnp.inf)
                    blk[...] = score
                    hb_ref[pl.ds(ks, sk), :] = score.astype(_COARSE_DTYPE)
        return carry

    lax.fori_loop(0, nkb, score_body, 0)

    def for_offsets(fn, init):
        def body(kb, carry):
            ks = pl.multiple_of(kb * sk, sk)
            for j in range(sk // rows):
                carry = fn(carry, pl.multiple_of(ks + j * rows, rows))
            return carry
        return lax.fori_loop(0, nkb, body, init)

    def for_tiles(fn, init):
        return for_offsets(lambda carry, off: fn(carry, sc_ref[pl.ds(off, rows), :], off), init)

    def count(indicator):
        acc = for_tiles(lambda acc, tile, off: acc + indicator(tile, off), jnp.zeros((rows, tq), jnp.float32))
        return jnp.sum(acc, axis=0, keepdims=True)

    keep_all = t_row < top_k

    def coarse_step(it, key16):
        trial = key16 | lax.shift_left(jnp.int32(1), 15 - it)
        cand = jnp.broadcast_to(_key16_to_f32(trial), (rows, tq)).astype(_COARSE_DTYPE)
        def add_tile(acc, off):
            ge = hb_ref[pl.ds(off, rows), :] >= cand
            return acc + jnp.where(ge, _COARSE_DTYPE(1), _COARSE_DTYPE(0))
        acc = for_offsets(add_tile, jnp.zeros((rows, tq), _COARSE_DTYPE))
        cnt = jnp.sum(acc.astype(jnp.float32), axis=0, keepdims=True)
        return jnp.where(cnt >= top_k, trial, key16)

    key16 = lax.fori_loop(0, 16, coarse_step, jnp.zeros((1, tq), jnp.int32))
    t16_bits = lax.bitcast_convert_type(_key16_to_f32(key16), jnp.int32)
    t16_key = jnp.where(t16_bits < 0, ~t16_bits, t16_bits | jnp.int32(_I32_MIN))
    key_base = t16_key - _BF16_KEY_STEP

    def fine_step(it, carry):
        offset, cnt_key = carry
        trial = offset | lax.shift_left(jnp.int32(1), _FINE_STEPS - 1 - it)
        cand = jnp.broadcast_to(_key_to_f32(key_base + trial), (rows, tq))
        cnt = count(lambda tile, off: jnp.where(tile >= cand, 1.0, 0.0))
        ok = cnt >= top_k
        return jnp.where(ok, trial, offset), jnp.where(ok, cnt, cnt_key)

    state = (jnp.zeros((1, tq), jnp.int32), jnp.full((1, tq), float(seq), jnp.float32))
    state = lax.fori_loop(0, _FINE_ALWAYS, fine_step, state)
    for start in range(_FINE_ALWAYS, _FINE_STEPS, _FINE_GROUP):
        stop = min(start + _FINE_GROUP, _FINE_STEPS)
        unresolved = jnp.where(keep_all, 0.0, jnp.where(state[1] > top_k, 1.0, 0.0))
        state = lax.cond(
            jnp.max(unresolved) > 0.0,
            lambda st, start=start, stop=stop: lax.fori_loop(start, stop, fine_step, st),
            lambda st: st,
            state)
    offset, n_ge = state
    tau = jnp.where(keep_all, -jnp.inf, _key_to_f32(key_base + offset))
    tau_b = jnp.broadcast_to(tau, (rows, tq))

    excess = jnp.where(keep_all, 0.0, jnp.where(n_ge > top_k, 1.0, 0.0))
    sub_row = lax.broadcasted_iota(jnp.int32, (rows, tq), 0)
    t_b = jnp.broadcast_to(t_row, (rows, tq))

    def write_with_ties():
        def mark_tied(acc, tile, off):
            eq = jnp.where(tile == tau_b, 1.0, 0.0)
            hb_ref[pl.ds(off, rows), :] = eq.astype(_COARSE_DTYPE)
            return acc + eq
        n_eq = jnp.sum(for_tiles(mark_tied, jnp.zeros((rows, tq), jnp.float32)), axis=0, keepdims=True)
        need = top_k - (n_ge - n_eq)
        r = lax.broadcasted_iota(jnp.int32, (sk, sk), 0)
        c = lax.broadcasted_iota(jnp.int32, (sk, sk), 1)
        tri = jnp.where(c <= r, 1.0, 0.0).astype(_COARSE_DTYPE)
        tau_blk = jnp.broadcast_to(tau, (sk, tq))
        def block(kb, taken):
            ks = pl.multiple_of(kb * sk, sk)
            rank = taken + jnp.dot(tri, hb_ref[pl.ds(ks, sk), :], preferred_element_type=jnp.float32)
            tile = sc_ref[pl.ds(ks, sk), :]
            tied = jnp.where(tile == tau_blk, jnp.where(rank <= need, 0.0, _NEG), _NEG)
            s_col = ks + lax.broadcasted_iota(jnp.int32, (sk, tq), 0)
            sc_ref[pl.ds(ks, sk), :] = jnp.where(s_col <= t_row, jnp.where(tile > tau_blk, 0.0, tied), _NEG)
            return rank[sk - 1:sk, :]
        lax.fori_loop(0, nkb, block, jnp.zeros((1, tq), jnp.float32))

    def write_without_ties():
        def write_bias(carry, tile, off):
            sc_ref[pl.ds(off, rows), :] = jnp.where(
                off + sub_row <= t_b, jnp.where(tile >= tau_b, 0.0, _NEG), _NEG)
            return carry
        for_tiles(write_bias, 0)

    lax.cond(jnp.max(excess) > 0.0, write_with_ties, write_without_ties)


def _attend(q_ref, k_ref, vt_ref, bias_ref, o_ref, m_sc, acc_sc, s_sc):
    tq = q_ref.shape[1]
    seq = k_ref.shape[0]
    sk = min(_SK_ATT, seq)
    i = pl.program_id(1)
    nkb = (i * tq + tq + sk - 1) // sk

    m_sc[...] = jnp.full(m_sc.shape, _NEG, jnp.float32)
    acc_sc[...] = jnp.zeros(acc_sc.shape, jnp.float32)

    def logits(kb, h):
        ks = pl.multiple_of(kb * sk, sk)
        hs = slice(h * ATTN_HEAD_DIM, (h + 1) * ATTN_HEAD_DIM)
        s = jnp.dot(k_ref[pl.ds(ks, sk), hs], q_ref[hs, :],
                    preferred_element_type=jnp.float32) + bias_ref[pl.ds(ks, sk), :]
        s_sc[h % 2] = s
        return jnp.max(s, axis=0, keepdims=True)

    def body(kb, mx_next):
        ks = pl.multiple_of(kb * sk, sk)
        kb_next = jnp.minimum(kb + 1, nkb - 1)
        for h in range(ATTN_HEADS):
            mx = mx_next
            if h + 1 < ATTN_HEADS:
                mx_next = logits(kb, h + 1)
            else:
                mx_next = logits(kb_next, 0)
            m_old = m_sc[h]
            m_new = jnp.maximum(m_old, mx)
            alpha = jnp.exp2(m_old - m_new)
            p = jnp.exp2(s_sc[h % 2] - m_new).astype(_MXU_DTYPE)
            acc_sc[h] = alpha * acc_sc[h] + jnp.dot(vt_ref[kb, h], p, preferred_element_type=jnp.float32)
            m_sc[h] = m_new
        return mx_next

    lax.fori_loop(0, nkb, body, logits(0, 0))
    for h in range(ATTN_HEADS):
        hs = slice(h * ATTN_HEAD_DIM, (h + 1) * ATTN_HEAD_DIM)
        acc = acc_sc[h]
        out_t = acc[:ATTN_HEAD_DIM, :] / acc[ATTN_HEAD_DIM:ATTN_HEAD_DIM + 1, :]
        o_ref[:, hs] = jnp.transpose(out_t).astype(o_ref.dtype)


def _sparse_attn_kernel(qit_ref, kwq_ref, kwk_ref, q_ref, k_ref, vt_ref, o_ref,
                        sc_ref, hb_ref, m_sc, acc_sc, s_sc, *, top_k):
    _select_bias(qit_ref, kwq_ref, kwk_ref, sc_ref, hb_ref, top_k)
    _attend(q_ref, k_ref, vt_ref, sc_ref, o_ref, m_sc, acc_sc, s_sc)


def _sparse_attn_call(qidx, kw, q, k, vt, bsz, seq, top_k):
    tq = min(_TQ_ATT, seq)
    nq = seq // tq
    sk = min(_SK_ATT, seq)
    assert vt.shape[-1] == sk and q.shape[-1] == tq and qidx.shape[-1] == tq
    per_batch = lambda shape: pl.BlockSpec(shape, lambda b, i: (b,) + (0,) * (len(shape) - 1),
                                           pipeline_mode=pl.Buffered(1))
    return pl.pallas_call(
        functools.partial(_sparse_attn_kernel, top_k=top_k),
        grid=(bsz, nq),
        in_specs=[
            pl.BlockSpec((None, IDX_HEADS * IDX_DIM, tq), lambda b, i: (b * nq + i, 0, 0)),
            pl.BlockSpec((tq, LANES), lambda b, i: (b * nq + i, 0)),
            per_batch((seq, LANES)),
            pl.BlockSpec((None, ATTN_WIDTH, tq), lambda b, i: (b * nq + i, 0, 0)),
            pl.BlockSpec((seq, ATTN_WIDTH), lambda b, i: (b, 0)),
            per_batch((seq // sk, ATTN_HEADS, _VT_ROWS, sk)),
        ],
        out_specs=pl.BlockSpec((tq, ATTN_WIDTH), lambda b, i: (b * nq + i, 0)),
        out_shape=jax.ShapeDtypeStruct((bsz * seq, ATTN_WIDTH), _MXU_DTYPE),
        scratch_shapes=[
            pltpu.VMEM((seq, tq), jnp.float32),
            pltpu.VMEM((seq, tq), _COARSE_DTYPE),
            pltpu.VMEM((ATTN_HEADS, 1, tq), jnp.float32),
            pltpu.VMEM((ATTN_HEADS, _VT_ROWS, tq), jnp.float32),
            pltpu.VMEM((2, sk, tq), jnp.float32),
        ],
        compiler_params=pltpu.CompilerParams(
            dimension_semantics=("parallel", "parallel"), vmem_limit_bytes=VMEM_LIMIT_BYTES),
        name="sparse_attn",
    )(qidx, kw, kw, q, k, vt)


def _out_kernel(x_ref, ma_ref, gb_ref, b_ref, wb_ref, wo_ref, g1_ref, b1_ref,
                wup_ref, wdn_ref, g2_ref, b2_ref, o_ref, *, alpha):
    branch_b = jnp.dot(b_ref[...], wb_ref[...], preferred_element_type=jnp.float32)
    merged = ma_ref[...] + gb_ref[...] * branch_b
    y = alpha * x_ref[...] + jnp.dot(merged.astype(_MXU_DTYPE), wo_ref[...],
                                     preferred_element_type=jnp.float32)
    x1 = _layer_norm(y, g1_ref[...], b1_ref[...])
    x1b = x1.astype(_MXU_DTYPE)
    acc = jnp.zeros(x1.shape, jnp.float32)
    for c in range(FFN_DIM // _FFN_CHUNK):
        cs = slice(c * _FFN_CHUNK, (c + 1) * _FFN_CHUNK)
        h = jnp.square(jnp.maximum(jnp.dot(x1b, wup_ref[:, cs], preferred_element_type=jnp.float32), 0.0))
        acc = acc + jnp.dot(h.astype(_MXU_DTYPE), wdn_ref[cs, :], preferred_element_type=jnp.float32)
    o_ref[...] = _layer_norm(alpha * x1 + acc, g2_ref[...], b2_ref[...])


def _out_call(x2, ma, gb, battn, w_b, w_o, g1, b1, w_up, w_dn, g2, b2, alpha):
    n = x2.shape[0]
    tm = _TM_OUT
    row_spec = pl.BlockSpec((tm, D_MODEL), lambda i: (i, 0))
    return pl.pallas_call(
        functools.partial(_out_kernel, alpha=alpha),
        grid=(n // tm,),
        in_specs=[row_spec, row_spec, row_spec, row_spec,
                  _resident(w_b.shape), _resident(w_o.shape), _resident(g1.shape), _resident(b1.shape),
                  _resident(w_up.shape), _resident(w_dn.shape), _resident(g2.shape), _resident(b2.shape)],
        out_specs=row_spec,
        out_shape=jax.ShapeDtypeStruct((n, D_MODEL), jnp.float32),
        compiler_params=pltpu.CompilerParams(
            dimension_semantics=("parallel",), vmem_limit_bytes=VMEM_LIMIT_BYTES),
        name="out",
    )(x2, ma, gb, battn, w_b, w_o, g1, b1, w_up, w_dn, g2, b2)


def _pack_w_in(w_in):
    offs = [0]
    for width in IN_SPLITS:
        offs.append(offs[-1] + width)
    u, v, q, k, va, qi, ki, wi, ga, gb = (w_in[:, offs[j]:offs[j + 1]] for j in range(len(IN_SPLITS)))
    pad = jnp.zeros((w_in.shape[0], LANES - IDX_DIM - IDX_HEADS), w_in.dtype)
    packed = jnp.concatenate([u, v, k, ga, gb, ki, wi, pad], axis=1).astype(_MXU_DTYPE)
    transposed = jnp.concatenate([va, q, qi], axis=1).T.astype(_MXU_DTYPE)
    return packed, transposed


def _layer(x, w_in, sgu_ln_g, sgu_ln_b, sgu_w, sgu_b, w_branch_a, w_branch_b, w_out,
           ln1_g, ln1_b, w_ffn_up, w_ffn_down, ln2_g, ln2_b, alpha):
    bsz, seq, _ = x.shape
    assert seq % _TM_PROJ == 0 and seq % _TM_OUT == 0 and _TM_PROJ % CHUNK == 0
    assert seq % min(_TQ_ATT, seq) == 0 and seq % min(_SK_ATT, seq) == 0
    assert seq % min(_TQ_IDX, seq) == 0 and seq % min(_SK_IDX, seq) == 0
    assert _TM_PROJ == min(_TQ_IDX, seq) == min(_TQ_ATT, seq) == min(_SK_ATT, seq)
    top_k = min(IDX_TOPK_MAX, seq // 4)
    x2 = x.reshape(bsz * seq, D_MODEL)
    vec = lambda a: a.reshape(1, -1)
    cast = lambda a: a.astype(_MXU_DTYPE)

    w_packed, w_t = _pack_w_in(w_in)
    q, k, vt, qidx, kw, ma, gb = _proj_call(
        x2, w_packed, w_t, vec(sgu_ln_g), vec(sgu_ln_b), sgu_w, sgu_b.T, cast(w_branch_a))
    battn = _sparse_attn_call(qidx, kw, q, k, vt, bsz, seq, top_k)
    out = _out_call(x2, ma, gb, battn, cast(w_branch_b), cast(w_out), vec(ln1_g), vec(ln1_b),
                    cast(w_ffn_up), cast(w_ffn_down), vec(ln2_g), vec(ln2_b), alpha)
    return out.reshape(bsz, seq, D_MODEL)


def kernel(x, w_in, sgu_ln_g, sgu_ln_b, sgu_w, sgu_b, w_branch_a, w_branch_b, w_out,
           ln1_g, ln1_b, w_ffn_up, w_ffn_down, ln2_g, ln2_b):
    depth = w_in.shape[0]
    alpha = (2.0 * depth) ** 0.25
    for i in range(depth):
        x = _layer(x, w_in[i], sgu_ln_g[i], sgu_ln_b[i], sgu_w[i], sgu_b[i], w_branch_a[i],
                   w_branch_b[i], w_out[i], ln1_g[i], ln1_b[i], w_ffn_up[i], w_ffn_down[i],
                   ln2_g[i], ln2_b[i], alpha)
    return x
```
